```python
import jax, jax.numpy as jnp
from jax import lax
import numpy as np

D_MODEL = 2048
BATCH = 4
SEQ = 4096
DEPTH = 1
DEC_BATCH = 32
DEC_SEQ = 4
PAST_LEN = 16384
PAGE_SIZE = 128

N_META = 16
RET_HEADS = 8
RET_DK = D_MODEL // 16
RET_DV = D_MODEL // 8
RET_CHUNK = 128
SB_HEADS = 16
SB_DH = D_MODEL // 32
SB_BLOCK = 128
SB_BIAS_HI = 4.0
SB_BIAS_SPAN = 6.0
D_FF = (D_MODEL * 11) // 4
CONV_W = 3
ROPE_BASE = 10000.0
EPS = 1e-6
GN_EPS = 1e-5

RET_QK = RET_HEADS * RET_DK
RET_V = RET_HEADS * RET_DV
SB_W = SB_HEADS * SB_DH
IN_WIDTHS = (RET_QK, RET_QK, RET_V, RET_V, SB_W, SB_W, SB_W, D_MODEL, D_MODEL)
IN_WIDTH = sum(IN_WIDTHS)
SPLIT_POINTS = tuple(int(v) for v in np.cumsum(IN_WIDTHS)[:-1])

kernel_name = "retention_stickbreaking_convffn_hybrid_step"

f32 = jnp.float32


def rmsnorm(x, g):
    xf = x.astype(f32)
    y = xf * lax.rsqrt(jnp.mean(xf * xf, axis=-1, keepdims=True) + EPS)
    return (y * g.astype(f32)).astype(x.dtype)


def group_norm_heads(o):
    mu = jnp.mean(o, axis=-1, keepdims=True)
    var = jnp.mean(jnp.square(o - mu), axis=-1, keepdims=True)
    return (o - mu) * lax.rsqrt(var + GN_EPS)


def rope(x, pos):
    half = x.shape[-1] // 2
    inv = ROPE_BASE ** (-jnp.arange(half, dtype=f32) / half)
    ang = pos.astype(f32)[:, None] * inv[None, :]
    c = jnp.cos(ang)[None, :, None, :]
    s = jnp.sin(ang)[None, :, None, :]
    x1, x2 = x[..., :half], x[..., half:]
    return jnp.concatenate([x1 * c - x2 * s, x1 * s + x2 * c], axis=-1)


def log_gamma():
    return jnp.log1p(-jnp.exp2(-5.0 - jnp.arange(RET_HEADS, dtype=f32)))


def retention_chunk(q, k, v, s):
    n = q.shape[1]
    lg = log_gamma()
    idx = jnp.arange(n)
    diff = idx[:, None] - idx[None, :]
    decay = jnp.where(diff >= 0, jnp.exp(jnp.maximum(diff, 0).astype(f32)[None] * lg[:, None, None]), 0.0)
    att = jnp.einsum('blhd,bmhd->bhlm', q, k) * decay[None]
    inner = jnp.einsum('bhlm,bmhe->blhe', att, v)
    q_decay = jnp.exp((idx + 1).astype(f32)[:, None] * lg[None, :])
    cross = jnp.einsum('blhd,bhde->blhe', q * q_decay[None, :, :, None], s.astype(f32))
    k_decay = jnp.exp((n - 1 - idx).astype(f32)[:, None] * lg[None, :])
    s_new = jnp.exp(n * lg)[None, :, None, None] * s.astype(f32) + jnp.einsum(
        'blhd,blhe->bhde', k * k_decay[None, :, :, None], v)
    return inner + cross, s_new


def retention_prompt(q, k, v):
    b = q.shape[0]
    s0 = jnp.zeros((b, RET_HEADS, RET_DK, RET_DV), f32)
    o_meta, s = retention_chunk(q[:, :N_META], k[:, :N_META], v[:, :N_META], s0)

    def to_chunks(a):
        a = a[:, N_META:]
        nc = a.shape[1] // RET_CHUNK
        return jnp.moveaxis(a.reshape(b, nc, RET_CHUNK, *a.shape[2:]), 1, 0)

    def step(carry, xs):
        o, s_new = retention_chunk(xs[0], xs[1], xs[2], carry)
        return s_new, o

    s_fin, o_chunks = lax.scan(step, s, (to_chunks(q), to_chunks(k), to_chunks(v)))
    o_real = jnp.moveaxis(o_chunks, 0, 1).reshape(b, -1, RET_HEADS, RET_DV)
    return jnp.concatenate([o_meta, o_real], axis=1), s_fin


def stick_breaking(q, q_pos, k, v, k_pos, bias):
    z = jnp.einsum('bqhd,bkhd->bhqk', q.astype(f32), k.astype(f32)) * (SB_DH ** -0.5) \
        + bias.astype(f32)[None, :, None, None]
    vis = (k_pos[None, :] < q_pos[:, None])[None, None]
    log_keep = jnp.where(vis, jax.nn.log_sigmoid(-z), 0.0)
    between = lax.cumsum(log_keep, axis=3, reverse=True) - log_keep
    w = jnp.where(vis, jnp.exp(jax.nn.log_sigmoid(z) + between), 0.0)
    return jnp.einsum('bhqk,bkhd->bqhd', w, v.astype(f32))


def sb_prompt(q, k, v, bias):
    b, t = k.shape[0], k.shape[1]
    k_pos = jnp.arange(t)
    meta_pos = jnp.arange(N_META)
    o_meta = stick_breaking(q[:, :N_META], meta_pos, k[:, :N_META], v[:, :N_META], meta_pos, bias)
    nb = (t - N_META) // SB_BLOCK
    qb = jnp.moveaxis(q[:, N_META:].reshape(b, nb, SB_BLOCK, SB_HEADS, SB_DH), 1, 0)
    pb = (N_META + jnp.arange(t - N_META)).reshape(nb, SB_BLOCK)
    ob = lax.map(lambda a: stick_breaking(a[0], a[1], k, v, k_pos, bias), (qb, pb))
    o_real = jnp.moveaxis(ob, 0, 1).reshape(b, t - N_META, SB_HEADS, SB_DH)
    return jnp.concatenate([o_meta, o_real], axis=1)


def project(h, w_in, pos):
    rq, rk, rv, rg, sq, sk, sv, gr, gs = jnp.split(h @ w_in, SPLIT_POINTS, axis=-1)
    b, t = h.shape[0], h.shape[1]
    rq = rope(rq.reshape(b, t, RET_HEADS, RET_DK).astype(f32), pos)
    rk = rope(rk.reshape(b, t, RET_HEADS, RET_DK).astype(f32), pos) * (RET_DK ** -0.5)
    rv = rv.reshape(b, t, RET_HEADS, RET_DV).astype(f32)
    sq = sq.reshape(b, t, SB_HEADS, SB_DH)
    sk = sk.reshape(b, t, SB_HEADS, SB_DH)
    sv = sv.reshape(b, t, SB_HEADS, SB_DH)
    return rq, rk, rv, rg, sq, sk, sv, gr, gs


def merge(ret_o, rg, sb_o, gr, gs, w_ret_proj, w_sb_proj, w_o):
    b, t = rg.shape[0], rg.shape[1]
    dt = rg.dtype
    ret_br = (jax.nn.silu(rg.astype(f32)) * group_norm_heads(ret_o).reshape(b, t, RET_V)).astype(dt) @ w_ret_proj
    sb_br = sb_o.reshape(b, t, SB_W).astype(dt) @ w_sb_proj
    m = jax.nn.sigmoid(gr.astype(f32)) * ret_br.astype(f32) + jax.nn.sigmoid(gs.astype(f32)) * sb_br.astype(f32)
    return m.astype(dt) @ w_o


def conv_ffn(u_ext, conv_w, w_down):
    n = u_ext.shape[1] - (CONV_W - 1)
    c = sum(conv_w[j] * u_ext[:, j:j + n] for j in range(CONV_W))
    a, g = jnp.split(c.astype(f32), 2, axis=-1)
    return (jax.nn.gelu(a) * g).astype(u_ext.dtype) @ w_down


def layer(xp, xs, ck, cv, page_table, s_ret, s_conv, norm1_g, w_in, sb_bias, w_ret_proj, w_sb_proj, w_o,
          norm2_g, w_up, conv_w, w_down):
    b, t = xp.shape[0], xp.shape[1]
    pos_p = jnp.arange(t)
    rq, rk, rv, rg, sq, sk, sv, gr, gs = project(rmsnorm(xp, norm1_g), w_in, pos_p)
    ret_o, ret_p = retention_prompt(rq, rk, rv)
    sb_o = sb_prompt(sq, sk, sv, sb_bias)
    xp = xp + merge(ret_o, rg, sb_o, gr, gs, w_ret_proj, w_sb_proj, w_o)
    u = rmsnorm(xp, norm2_g) @ w_up
    u_ext = jnp.concatenate([jnp.zeros((b, CONV_W - 1, u.shape[-1]), u.dtype), u], axis=1)
    xp = xp + conv_ffn(u_ext, conv_w, w_down)
    conv_p = u_ext[:, -(CONV_W - 1):]
    k_p, v_p = sk, sv

    db, ds = xs.shape[0], xs.shape[1]
    past = page_table.shape[1] * ck.shape[1]
    pos_s = past + jnp.arange(ds)
    rq, rk, rv, rg, sq, sk, sv, gr, gs = project(rmsnorm(xs, norm1_g), w_in, pos_s)
    ret_o, ret_s = retention_chunk(rq, rk, rv, s_ret)
    k_past = ck[page_table].reshape(db, past, SB_HEADS, SB_DH)
    v_past = cv[page_table].reshape(db, past, SB_HEADS, SB_DH)
    k_all = jnp.concatenate([k_past.astype(f32), sk.astype(f32)], axis=1)
    v_all = jnp.concatenate([v_past.astype(f32), sv.astype(f32)], axis=1)
    sb_o = stick_breaking(sq, pos_s, k_all, v_all, jnp.arange(past + ds), sb_bias)
    xs = xs + merge(ret_o, rg, sb_o, gr, gs, w_ret_proj, w_sb_proj, w_o)
    u = rmsnorm(xs, norm2_g) @ w_up
    u_ext = jnp.concatenate([s_conv.astype(u.dtype), u], axis=1)
    xs = xs + conv_ffn(u_ext, conv_w, w_down)
    conv_s = u_ext[:, -(CONV_W - 1):]
    return xp, xs, (k_p, v_p, sk, sv, ret_p, ret_s, conv_p, conv_s)


def setup_inputs(seed: int = 0) -> dict:
    key = jax.random.key(seed)
    ks = jax.random.split(key, 20)
    n_pages = PAST_LEN // PAGE_SIZE
    n_pool = (5 * DEC_BATCH * n_pages) // 4
    nrm = jax.random.normal
    page_table = jax.random.permutation(ks[0], n_pool)[:DEC_BATCH * n_pages].reshape(DEC_BATCH, n_pages).astype(jnp.int32)
    head_frac = jnp.arange(SB_HEADS, dtype=f32) / (SB_HEADS - 1)
    sb_bias = -(SB_BIAS_HI + SB_BIAS_SPAN * head_frac)[None, :] + 0.1 * nrm(ks[18], (DEPTH, SB_HEADS), f32)
    return {
        'x_prompt': nrm(ks[1], (BATCH, SEQ, D_MODEL), f32),
        'x_sample': nrm(ks[2], (DEC_BATCH, DEC_SEQ, D_MODEL), f32),
        'cache_k': nrm(ks[3], (DEPTH, n_pool, PAGE_SIZE, SB_HEADS, SB_DH), f32),
        'cache_v': nrm(ks[4], (DEPTH, n_pool, PAGE_SIZE, SB_HEADS, SB_DH), f32),
        'page_table': page_table,
        'state_ret': 0.5 * nrm(ks[5], (DEPTH, DEC_BATCH, RET_HEADS, RET_DK, RET_DV), f32),
        'state_conv': nrm(ks[6], (DEPTH, DEC_BATCH, CONV_W - 1, 2 * D_FF), f32),
        'meta_tokens': nrm(ks[7], (N_META, D_MODEL), f32),
        'norm1_g': 1.0 + 0.02 * nrm(ks[8], (DEPTH, D_MODEL), f32),
        'w_in': nrm(ks[9], (DEPTH, D_MODEL, IN_WIDTH), f32) * D_MODEL ** -0.5,
        'sb_bias': sb_bias,
        'w_ret_proj': nrm(ks[10], (DEPTH, RET_V, D_MODEL), f32) * RET_V ** -0.5,
        'w_sb_proj': nrm(ks[11], (DEPTH, SB_W, D_MODEL), f32) * SB_W ** -0.5,
        'w_o': nrm(ks[12], (DEPTH, D_MODEL, D_MODEL), f32) * D_MODEL ** -0.5,
        'norm2_g': 1.0 + 0.02 * nrm(ks[13], (DEPTH, D_MODEL), f32),
        'w_up': nrm(ks[14], (DEPTH, D_MODEL, 2 * D_FF), f32) * D_MODEL ** -0.5,
        'conv_w': nrm(ks[15], (DEPTH, CONV_W, 2 * D_FF), f32) * CONV_W ** -0.5,
        'w_down': nrm(ks[16], (DEPTH, D_FF, D_MODEL), f32) * D_FF ** -0.5,
        'norm_f_g': 1.0 + 0.02 * nrm(ks[17], (D_MODEL,), f32),
    }


def reference(x_prompt, x_sample, cache_k, cache_v, page_table, state_ret, state_conv, meta_tokens,
              norm1_g, w_in, sb_bias, w_ret_proj, w_sb_proj, w_o, norm2_g, w_up, conv_w, w_down, norm_f_g):
    b = x_prompt.shape[0]
    meta = jnp.broadcast_to(meta_tokens.astype(x_prompt.dtype)[None], (b, N_META, D_MODEL))
    xp = jnp.concatenate([meta, x_prompt], axis=1)
    xs = x_sample
    outs = []
    for l in range(DEPTH):
        xp, xs, st = layer(xp, xs, cache_k[l], cache_v[l], page_table, state_ret[l], state_conv[l],
                           norm1_g[l], w_in[l], sb_bias[l], w_ret_proj[l], w_sb_proj[l], w_o[l], norm2_g[l],
                           w_up[l], conv_w[l], w_down[l])
        outs.append(st)
    y_prompt = rmsnorm(xp[:, N_META:], norm_f_g)
    y_sample = rmsnorm(xs, norm_f_g)
    k_p, v_p, k_s, v_s, ret_p, ret_s, conv_p, conv_s = [jnp.stack([o[i] for o in outs]) for i in range(8)]
    return (y_prompt, y_sample, k_p, v_p, k_s, v_s, ret_p, ret_s, conv_p, conv_s)
```

```python
import functools

import jax
import jax.numpy as jnp
from jax import lax
from jax.experimental import pallas as pl
from jax.experimental.pallas import tpu as pltpu

f32 = jnp.float32
bf16 = jnp.bfloat16

D_MODEL = 2048
N_META = 16
RET_HEADS = 8
RET_DK = 128
RET_DV = 256
SB_HEADS = 16
SB_DH = 64
D_FF = 5632
CONV_W = 3
ROPE_BASE = 10000.0
EPS = 1e-6
GN_EPS = 1e-5

RET_QK = RET_HEADS * RET_DK
RET_V = RET_HEADS * RET_DV
SB_W = SB_HEADS * SB_DH
C_RQ, C_RK, C_RV, C_RG = 0, RET_QK, 2 * RET_QK, 2 * RET_QK + RET_V
C_SQ = C_RG + RET_V
C_SK, C_SV = C_SQ + SB_W, C_SQ + 2 * SB_W
C_GR = C_SV + SB_W
C_GS = C_GR + D_MODEL
IN_WIDTH = C_GS + D_MODEL

RET_CHUNK = 128
SB_BQ = 256
SB_BK = 256
SMALL_ROWS = 16
FF_CHUNK = 512
VMEM_LIMIT = 56 * 1024 * 1024


def _cparams(sem):
    return pltpu.CompilerParams(dimension_semantics=sem, vmem_limit_bytes=VMEM_LIMIT)


def _dot(a, b):
    return jnp.dot(a, b, preferred_element_type=f32)


def _dot_nt(a, b):
    return lax.dot_general(a, b, (((1,), (1,)), ((), ())), preferred_element_type=f32)


def _sigmoid(x):
    return 1.0 / (1.0 + jnp.exp(-x))


def _rmsnorm(x, g):
    return x * lax.rsqrt(jnp.mean(x * x, axis=-1, keepdims=True) + EPS) * g


def _norm_mm_kernel(x_ref, g_ref, w_ref, o_ref, h_ref):
    @pl.when(pl.program_id(1) == 0)
    def _():
        h_ref[...] = _rmsnorm(x_ref[...], g_ref[...]).astype(bf16)

    o_ref[...] = _dot(h_ref[...], w_ref[...])


def _norm_mm(x, g, w, tm, tn):
    m, d = x.shape
    n = w.shape[1]
    return pl.pallas_call(
        _norm_mm_kernel,
        out_shape=jax.ShapeDtypeStruct((m, n), f32),
        grid=(m // tm, n // tn),
        in_specs=[pl.BlockSpec((tm, d), lambda i, j: (i, 0)),
                  pl.BlockSpec((1, d), lambda i, j: (0, 0)),
                  pl.BlockSpec((d, tn), lambda i, j: (0, j))],
        out_specs=pl.BlockSpec((tm, tn), lambda i, j: (i, j)),
        scratch_shapes=[pltpu.VMEM((tm, d), bf16)],
        compiler_params=_cparams(("parallel", "arbitrary")),
        name="norm_in_proj",
    )(x, g, w)


def _rope(x, cos2, sin2):
    return x * cos2 + pltpu.roll(x, RET_DK // 2, 1) * sin2


def _gated_groupnorm(o, gate):
    mu = jnp.mean(o, axis=-1, keepdims=True)
    oc = o - mu
    var = jnp.mean(oc * oc, axis=-1, keepdims=True)
    return (gate * _sigmoid(gate)) * (oc * lax.rsqrt(var + GN_EPS))


def _ret_prompt_kernel(q_ref, k_ref, v_ref, g_ref, cos_ref, sin_ref, lg_ref, s0_ref, o_ref, sfin_ref, s_ref):
    c = pl.program_id(2)
    n = RET_CHUNK

    @pl.when(c == 0)
    def _():
        s_ref[...] = s0_ref[...]

    lgv = lg_ref[...]
    lg = lgv[:, :RET_DK]
    row = lax.broadcasted_iota(jnp.int32, (n, RET_DK), 0).astype(f32)
    col = lax.broadcasted_iota(jnp.int32, (n, n), 1).astype(f32)
    cos2, sin2 = cos_ref[...], sin_ref[...]
    q = _rope(q_ref[...], cos2, sin2)
    k = _rope(k_ref[...], cos2, sin2) * (RET_DK ** -0.5)
    diff = row - col
    decay = jnp.where(diff >= 0, jnp.exp(diff * lg), 0.0)
    att = _dot_nt(q.astype(bf16), k.astype(bf16)) * decay
    vb = v_ref[...].astype(bf16)
    s = s_ref[...]
    o = _dot(att.astype(bf16), vb) + _dot((q * jnp.exp((row + 1.0) * lg)).astype(bf16), s.astype(bf16))
    kd = k * jnp.exp((n - 1.0 - row) * lg)
    s_ref[...] = jnp.exp(n * lgv) * s + _dot(kd.T.astype(bf16), vb)
    o_ref[...] = _gated_groupnorm(o, g_ref[...]).astype(bf16)

    @pl.when(c == pl.num_programs(2) - 1)
    def _():
        sfin_ref[...] = s_ref[...]


def _ret_prompt(proj, cos2, sin2, lg, s0, batch, seq):
    nc = seq // RET_CHUNK
    n = RET_CHUNK
    row = lambda b, h, c: b * nc + c
    return pl.pallas_call(
        _ret_prompt_kernel,
        out_shape=(jax.ShapeDtypeStruct((batch * seq, RET_V), bf16),
                   jax.ShapeDtypeStruct((batch, RET_HEADS, RET_DK, RET_DV), f32)),
        grid=(batch, RET_HEADS, nc),
        in_specs=[pl.BlockSpec((n, RET_DK), lambda b, h, c: (row(b, h, c), C_RQ // RET_DK + h)),
                  pl.BlockSpec((n, RET_DK), lambda b, h, c: (row(b, h, c), C_RK // RET_DK + h)),
                  pl.BlockSpec((n, RET_DV), lambda b, h, c: (row(b, h, c), C_RV // RET_DV + h)),
                  pl.BlockSpec((n, RET_DV), lambda b, h, c: (row(b, h, c), C_RG // RET_DV + h)),
                  pl.BlockSpec((n, RET_DK), lambda b, h, c: (c, 0)),
                  pl.BlockSpec((n, RET_DK), lambda b, h, c: (c, 0)),
                  pl.BlockSpec((None, 1, RET_DV), lambda b, h, c: (h, 0, 0)),
                  pl.BlockSpec((None, RET_DK, RET_DV), lambda b, h, c: (h, 0, 0))],
        out_specs=(pl.BlockSpec((n, RET_DV), lambda b, h, c: (row(b, h, c), h)),
                   pl.BlockSpec((None, None, RET_DK, RET_DV), lambda b, h, c: (b, h, 0, 0))),
        scratch_shapes=[pltpu.VMEM((RET_DK, RET_DV), f32)],
        compiler_params=_cparams(("parallel", "parallel", "arbitrary")),
        name="retention_prompt",
    )(proj, proj, proj, proj, cos2, sin2, lg, s0)


def _ret_small_kernel(q_ref, k_ref, v_ref, g_ref, cos_ref, sin_ref, lg_ref, s0_ref, o_ref, s_ref, *, n_tok):
    r = SMALL_ROWS
    n_seq = r // n_tok
    ri = lax.broadcasted_iota(jnp.int32, (r, RET_DK), 0)
    ti = (ri % n_tok).astype(f32)
    rj = lax.broadcasted_iota(jnp.int32, (r, 128), 1)
    rr = lax.broadcasted_iota(jnp.int32, (r, 128), 0)
    tdiff = ((rr % n_tok) - (rj % n_tok)).astype(f32)
    visible = ((rr // n_tok) == (rj // n_tok)) & (tdiff >= 0)
    rk = lax.broadcasted_iota(jnp.int32, (128, RET_DK), 0)
    tk = (rk % n_tok).astype(f32)
    rv = lax.broadcasted_iota(jnp.int32, (r, RET_DV), 0)
    cos2, sin2 = cos_ref[...], sin_ref[...]
    pad = 128 - r
    for h in range(RET_HEADS):
        lgv = lg_ref[h]
        lg = lgv[:, :RET_DK]
        q = _rope(q_ref[:, h * RET_DK:(h + 1) * RET_DK], cos2, sin2)
        k = _rope(k_ref[:, h * RET_DK:(h + 1) * RET_DK], cos2, sin2) * (RET_DK ** -0.5)
        kpad = jnp.concatenate([k, jnp.zeros((pad, RET_DK), f32)], axis=0)
        vpad = jnp.concatenate([v_ref[:, h * RET_DV:(h + 1) * RET_DV], jnp.zeros((pad, RET_DV), f32)],
                               axis=0).astype(bf16)
        decay = jnp.where(visible, jnp.exp(tdiff * lg), 0.0)
        att = _dot_nt(q.astype(bf16), kpad.astype(bf16)) * decay
        o = _dot(att.astype(bf16), vpad)
        qd = (q * jnp.exp((ti + 1.0) * lg)).astype(bf16)
        kd = kpad * jnp.exp((n_tok - 1.0 - tk) * lg)
        for s_i in range(n_seq):
            s = s0_ref[s_i, h]
            mine_k = (rk // n_tok) == s_i
            mine_v = (rv // n_tok) == s_i
            o = o + jnp.where(mine_v, _dot(qd, s.astype(bf16)), 0.0)
            s_ref[s_i, h] = jnp.exp(n_tok * lgv) * s + _dot(jnp.where(mine_k, kd, 0.0).T.astype(bf16), vpad)
        o_ref[:, h * RET_DV:(h + 1) * RET_DV] = _gated_groupnorm(
            o, g_ref[:, h * RET_DV:(h + 1) * RET_DV]).astype(bf16)


def _ret_small(proj, row0, n_rows, n_tok, cos2, sin2, lg, s0):
    r = SMALL_ROWS
    n_seq = r // n_tok
    steps = n_rows // r
    b0 = row0 // r
    return pl.pallas_call(
        functools.partial(_ret_small_kernel, n_tok=n_tok),
        out_shape=(jax.ShapeDtypeStruct((n_rows, RET_V), bf16),
                   jax.ShapeDtypeStruct(s0.shape, f32)),
        grid=(steps,),
        in_specs=[pl.BlockSpec((r, RET_QK), lambda i: (b0 + i, C_RQ // RET_QK)),
                  pl.BlockSpec((r, RET_QK), lambda i: (b0 + i, C_RK // RET_QK)),
                  pl.BlockSpec((r, RET_V), lambda i: (b0 + i, C_RV // RET_V)),
                  pl.BlockSpec((r, RET_V), lambda i: (b0 + i, C_RG // RET_V)),
                  pl.BlockSpec((r, RET_DK), lambda i: (0, 0)),
                  pl.BlockSpec((r, RET_DK), lambda i: (0, 0)),
                  pl.BlockSpec((RET_HEADS, 1, RET_DV), lambda i: (0, 0, 0)),
                  pl.BlockSpec((n_seq, RET_HEADS, RET_DK, RET_DV), lambda i: (i, 0, 0, 0))],
        out_specs=(pl.BlockSpec((r, RET_V), lambda i: (i, 0)),
                   pl.BlockSpec((n_seq, RET_HEADS, RET_DK, RET_DV), lambda i: (i, 0, 0, 0))),
        compiler_params=_cparams(("parallel",)),
        name="retention_small",
    )(proj, proj, proj, proj, cos2, sin2, lg, s0)


def _sb_weights(z, r_in, tri, mask):
    lk = -(jnp.maximum(z, 0.0) + jnp.log1p(jnp.exp(-jnp.abs(z))))
    if mask is not None:
        lk = jnp.where(mask, lk, 0.0)
    hi = lk.astype(bf16)
    lo = (lk - hi.astype(f32)).astype(bf16)
    between = _dot(hi, tri) + _dot(lo, tri)
    w = jnp.exp(z + lk + between + r_in)
    if mask is not None:
        w = jnp.where(mask, w, 0.0)
    return w.astype(bf16), r_in + between[:, :1] + lk[:, :1]


def _sb_prompt_kernel(bias_ref, q_ref, k_ref, v_ref, km_ref, vm_ref, tri_ref, o_ref, acc_ref, r_ref):
    p, i = pl.program_id(1), pl.program_id(2)
    bq, bk = SB_BQ, SB_BK
    q = q_ref[...] * (SB_DH ** -0.5)
    lane = lax.broadcasted_iota(jnp.int32, (bq, 2 * SB_DH), 1)
    first = lane < SB_DH
    qm = (jnp.where(first, q, 0.0).astype(bf16), jnp.where(first, 0.0, q).astype(bf16))
    bias = (bias_ref[2 * p], bias_ref[2 * p + 1])
    tri = tri_ref[...]

    def block(kb, vb, mask, tri_b, init):
        for hh in range(2):
            z = _dot_nt(qm[hh], kb) + bias[hh]
            r_in = jnp.zeros((bq, 1), f32) if init else r_ref[hh]
            w, r_out = _sb_weights(z, r_in, tri_b, mask)
            r_ref[hh] = r_out
            contrib = _dot(w, vb)
            acc_ref[hh] = contrib if init else acc_ref[hh] + contrib

    kk = lax.broadcasted_iota(jnp.int32, (bq, bk), 1)
    qq = lax.broadcasted_iota(jnp.int32, (bq, bk), 0)
    start = pl.multiple_of(i * bk, bk)
    block(k_ref[pl.ds(start, bk), :].astype(bf16), v_ref[pl.ds(start, bk), :].astype(bf16), kk < qq, tri, True)

    def body(t, carry):
        s0 = pl.multiple_of((i - 1 - t) * bk, bk)
        block(k_ref[pl.ds(s0, bk), :].astype(bf16), v_ref[pl.ds(s0, bk), :].astype(bf16), None, tri, False)
        return carry

    lax.fori_loop(0, i, body, 0)
    nm = km_ref.shape[0]
    km_valid = lax.broadcasted_iota(jnp.int32, (bq, nm), 1) < N_META
    block(km_ref[...], vm_ref[...], km_valid, tri[:nm, :nm], False)
    o_ref[...] = jnp.where(first, acc_ref[0], acc_ref[1]).astype(bf16)


def _sb_prompt(proj, km, vm, bias, tri, batch, seq):
    nq = seq // SB_BQ
    hp = 2 * SB_DH
    return pl.pallas_call(
        _sb_prompt_kernel,
        out_shape=jax.ShapeDtypeStruct((batch * seq, SB_W), bf16),
        grid_spec=pltpu.PrefetchScalarGridSpec(
            num_scalar_prefetch=1,
            grid=(batch, SB_HEADS // 2, nq),
            in_specs=[pl.BlockSpec((SB_BQ, hp), lambda b, p, i, s: (b * nq + i, C_SQ // hp + p)),
                      pl.BlockSpec((seq, hp), lambda b, p, i, s: (b, C_SK // hp + p)),
                      pl.BlockSpec((seq, hp), lambda b, p, i, s: (b, C_SV // hp + p)),
                      pl.BlockSpec((km.shape[0], hp), lambda b, p, i, s: (0, p)),
                      pl.BlockSpec((km.shape[0], hp), lambda b, p, i, s: (0, p)),
                      pl.BlockSpec((SB_BK, SB_BK), lambda b, p, i, s: (0, 0))],
            out_specs=pl.BlockSpec((SB_BQ, hp), lambda b, p, i, s: (b * nq + i, p)),
            scratch_shapes=[pltpu.VMEM((2, SB_BQ, hp), f32), pltpu.VMEM((2, SB_BQ, 1), f32)]),
        compiler_params=_cparams(("parallel", "parallel", "arbitrary")),
        name="stick_breaking_prompt",
    )(bias, proj, proj, proj, km, vm, tri)


def _sb_meta_kernel(bias_ref, q_ref, km_ref, vm_ref, tri_ref, o_ref):
    p = pl.program_id(0)
    nq, nm = q_ref.shape[0], km_ref.shape[0]
    q = q_ref[...] * (SB_DH ** -0.5)
    first = lax.broadcasted_iota(jnp.int32, (nq, 2 * SB_DH), 1) < SB_DH
    mask = lax.broadcasted_iota(jnp.int32, (nq, nm), 1) < lax.broadcasted_iota(jnp.int32, (nq, nm), 0)
    tri = tri_ref[...][:nm, :nm]
    kb, vb = km_ref[...], vm_ref[...]
    out = []
    for hh, qh in enumerate((jnp.where(first, q, 0.0), jnp.where(first, 0.0, q))):
        z = _dot_nt(qh.astype(bf16), kb) + bias_ref[2 * p + hh]
        w, _ = _sb_weights(z, jnp.zeros((nq, 1), f32), tri, mask)
        out.append(_dot(w, vb))
    o_ref[...] = jnp.where(first, out[0], out[1]).astype(bf16)


def _sb_meta(proj_small, row0, km, vm, bias, tri):
    hp = 2 * SB_DH
    nm = km.shape[0]
    return pl.pallas_call(
        _sb_meta_kernel,
        out_shape=jax.ShapeDtypeStruct((N_META, SB_W), bf16),
        grid_spec=pltpu.PrefetchScalarGridSpec(
            num_scalar_prefetch=1,
            grid=(SB_HEADS // 2,),
            in_specs=[pl.BlockSpec((N_META, hp), lambda p, s: (row0 // N_META, C_SQ // hp + p)),
                      pl.BlockSpec((nm, hp), lambda p, s: (0, p)),
                      pl.BlockSpec((nm, hp), lambda p, s: (0, p)),
                      pl.BlockSpec((SB_BK, SB_BK), lambda p, s: (0, 0))],
            out_specs=pl.BlockSpec((N_META, hp), lambda p, s: (0, p))),
        compiler_params=_cparams(("parallel",)),
        name="stick_breaking_meta",
    )(bias, proj_small, km, vm, tri)


def _sb_sample_kernel(pt_ref, q_ref, kn_ref, vn_ref, kp_ref, vp_ref, bias_ref, tri_ref, o_ref,
                      qbd_ref, acc_ref, r_ref, *, n_tok):
    j = pl.program_id(1)
    page = kp_ref.shape[0]
    rows = n_tok * SB_HEADS
    head_of_col = lax.broadcasted_iota(jnp.int32, (SB_HEADS, SB_W), 1) // SB_DH
    own = head_of_col == lax.broadcasted_iota(jnp.int32, (SB_HEADS, SB_W), 0)
    tri = tri_ref[...][:page, :page]

    def block(kb, vb, mask, init):
        z = _dot_nt(qbd_ref[...], kb) + bias_ref[...]
        r_in = jnp.zeros((rows, 1), f32) if init else r_ref[...]
        w, r_out = _sb_weights(z, r_in, tri, mask)
        r_ref[...] = r_out
        contrib = _dot(w, vb)
        acc_ref[...] = contrib if init else acc_ref[...] + contrib

    @pl.when(j == 0)
    def _():
        q = q_ref[...] * (SB_DH ** -0.5)
        for t in range(n_tok):
            qbd_ref[t * SB_HEADS:(t + 1) * SB_HEADS, :] = jnp.where(own, q[t:t + 1, :], 0.0).astype(bf16)
        kk = lax.broadcasted_iota(jnp.int32, (rows, page), 1)
        tq = lax.broadcasted_iota(jnp.int32, (rows, page), 0) // SB_HEADS
        block(kn_ref[...], vn_ref[...], kk < tq, True)

    @pl.when(j > 0)
    def _():
        block(kp_ref[...].astype(bf16), vp_ref[...].astype(bf16), None, False)

    @pl.when(j == pl.num_programs(1) - 1)
    def _():
        for t in range(n_tok):
            sel = jnp.where(own, acc_ref[t * SB_HEADS:(t + 1) * SB_HEADS, :], 0.0)
            o_ref[t:t + 1, :] = jnp.sum(sel, axis=0, keepdims=True)


def _sb_sample(page_table, q3, kn, vn, ck, cv, bias_col, tri):
    nb, n_tok, _ = q3.shape
    n_pages = page_table.shape[1]
    page = ck.shape[1]
    rows = n_tok * SB_HEADS
    page_idx = lambda b, j, pt: (pt[b, n_pages - jnp.maximum(j, 1)], 0, 0)
    return pl.pallas_call(
        functools.partial(_sb_sample_kernel, n_tok=n_tok),
        out_shape=jax.ShapeDtypeStruct((nb, n_tok, SB_W), f32),
        grid_spec=pltpu.PrefetchScalarGridSpec(
            num_scalar_prefetch=1,
            grid=(nb, n_pages + 1),
            in_specs=[pl.BlockSpec((None, n_tok, SB_W), lambda b, j, pt: (b, 0, 0)),
                      pl.BlockSpec((None, page, SB_W), lambda b, j, pt: (b, 0, 0)),
                      pl.BlockSpec((None, page, SB_W), lambda b, j, pt: (b, 0, 0)),
                      pl.BlockSpec((None, page, SB_W), page_idx),
                      pl.BlockSpec((None, page, SB_W), page_idx),
                      pl.BlockSpec((rows, 1), lambda b, j, pt: (0, 0)),
                      pl.BlockSpec((SB_BK, SB_BK), lambda b, j, pt: (0, 0))],
            out_specs=pl.BlockSpec((None, n_tok, SB_W), lambda b, j, pt: (b, 0, 0)),
            scratch_shapes=[pltpu.VMEM((rows, SB_W), bf16), pltpu.VMEM((rows, SB_W), f32),
                            pltpu.VMEM((rows, 1), f32)]),
        compiler_params=_cparams(("parallel", "arbitrary")),
        name="stick_breaking_sample",
    )(page_table, q3, kn, vn, ck, cv, bias_col, tri)


def _gate_mix_kernel(ret_ref, sb_ref, gr_ref, gs_ref, wr_ref, ws_ref, o_ref):
    m = _sigmoid(gr_ref[...]) * _dot(ret_ref[...], wr_ref[...]) + _sigmoid(gs_ref[...]) * _dot(sb_ref[...], ws_ref[...])
    o_ref[...] = m.astype(bf16)


def _gate_mix(ret, sb, proj, wr, ws, tm, tn):
    m = ret.shape[0]
    return pl.pallas_call(
        _gate_mix_kernel,
        out_shape=jax.ShapeDtypeStruct((m, D_MODEL), bf16),
        grid=(m // tm, D_MODEL // tn),
        in_specs=[pl.BlockSpec((tm, RET_V), lambda i, j: (i, 0)),
                  pl.BlockSpec((tm, SB_W), lambda i, j: (i, 0)),
                  pl.BlockSpec((tm, tn), lambda i, j: (i, C_GR // tn + j)),
                  pl.BlockSpec((tm, tn), lambda i, j: (i, C_GS // tn + j)),
                  pl.BlockSpec((RET_V, tn), lambda i, j: (0, j)),
                  pl.BlockSpec((SB_W, tn), lambda i, j: (0, j))],
        out_specs=pl.BlockSpec((tm, tn), lambda i, j: (i, j)),
        compiler_params=_cparams(("parallel", "arbitrary")),
        name="gate_mix",
    )(ret, sb, proj, proj, wr, ws)


def _out_proj_kernel(x_ref, m_ref, wo_ref, g_ref, x1_ref, h_ref):
    x1 = x_ref[...] + _dot(m_ref[...], wo_ref[...])
    x1_ref[...] = x1
    h_ref[...] = _rmsnorm(x1, g_ref[...]).astype(bf16)


def _out_proj(x, m, wo, g, tm):
    rows = x.shape[0]
    return pl.pallas_call(
        _out_proj_kernel,
        out_shape=(jax.ShapeDtypeStruct((rows, D_MODEL), f32), jax.ShapeDtypeStruct((rows, D_MODEL), bf16)),
        grid=(rows // tm,),
        in_specs=[pl.BlockSpec((tm, D_MODEL), lambda i: (i, 0)),
                  pl.BlockSpec((tm, D_MODEL), lambda i: (i, 0)),
                  pl.BlockSpec((D_MODEL, D_MODEL), lambda i: (0, 0)),
                  pl.BlockSpec((1, D_MODEL), lambda i: (0, 0))],
        out_specs=(pl.BlockSpec((tm, D_MODEL), lambda i: (i, 0)), pl.BlockSpec((tm, D_MODEL), lambda i: (i, 0))),
        compiler_params=_cparams(("parallel",)),
        name="out_proj_norm",
    )(x, m, wo, g)


def _gelu_tanh(x):
    return 0.5 * x * (1.0 + jnp.tanh(0.7978845608028654 * (x + 0.044715 * (x * x * x))))


def _ffn_finish(c, act, wd_ref, x_ref, gf_ref, y_ref, acc_ref):
    contrib = _dot(act.astype(bf16), wd_ref[...])

    @pl.when(c == 0)
    def _():
        acc_ref[...] = contrib

    @pl.when(c > 0)
    def _():
        acc_ref[...] += contrib

    @pl.when(c == pl.num_programs(1) - 1)
    def _():
        y_ref[...] = _rmsnorm(x_ref[...] + acc_ref[...], gf_ref[...])


def _ffn_prompt_kernel(h_ref, x_ref, wa_ref, wg_ref, cwa_ref, cwg_ref, pa_ref, pg_ref, wd_ref, gf_ref, y_ref, ta_ref, tg_ref,
                       acc_ref, prev_ref, *, tiles_per_seq, n_chunks):
    i, c = pl.program_id(0), pl.program_id(1)
    tm = h_ref.shape[0]
    h = h_ref[...]
    seq_start = (i % tiles_per_seq) == 0
    row = lax.broadcasted_iota(jnp.int32, (tm, FF_CHUNK), 0)

    def conv_half(w_ref, cw_ref, first_ref, slot, tail_ref):
        u = _dot(h, w_ref[...])
        prev = jnp.where(seq_start, first_ref[...], prev_ref[slot])
        u1 = jnp.where(row == 0, prev[7:8, :], pltpu.roll(u, 1, 0))
        u2 = jnp.where(row == 0, prev[6:7, :], jnp.where(row == 1, prev[7:8, :], pltpu.roll(u, 2, 0)))
        cw = cw_ref[...]
        tail = u[tm - 8:, :]
        prev_ref[slot] = tail
        tail_ref[...] = tail
        return cw[2:3, :] * u + cw[1:2, :] * u1 + cw[0:1, :] * u2

    a = conv_half(wa_ref, cwa_ref, pa_ref, c, ta_ref)
    g = conv_half(wg_ref, cwg_ref, pg_ref, n_chunks + c, tg_ref)
    _ffn_finish(c, _gelu_tanh(a) * g, wd_ref, x_ref, gf_ref, y_ref, acc_ref)


def _ffn_prompt(h, x1, wup, cw, first_a, first_g, wd, gf, tm, rows_per_seq):
    rows = h.shape[0]
    fc = FF_CHUNK
    nch = D_FF // fc
    nt = rows // tm
    return pl.pallas_call(
        functools.partial(_ffn_prompt_kernel, tiles_per_seq=rows_per_seq // tm, n_chunks=nch),
        out_shape=(jax.ShapeDtypeStruct((rows, D_MODEL), f32),
                   jax.ShapeDtypeStruct((nt, 8, D_FF), f32), jax.ShapeDtypeStruct((nt, 8, D_FF), f32)),
        grid=(nt, nch),
        in_specs=[pl.BlockSpec((tm, D_MODEL), lambda i, c: (i, 0)),
                  pl.BlockSpec((tm, D_MODEL), lambda i, c: (i, 0)),
                  pl.BlockSpec((D_MODEL, fc), lambda i, c: (0, c)),
                  pl.BlockSpec((D_MODEL, fc), lambda i, c: (0, nch + c)),
                  pl.BlockSpec((CONV_W, fc), lambda i, c: (0, c)),
                  pl.BlockSpec((CONV_W, fc), lambda i, c: (0, nch + c)),
                  pl.BlockSpec((8, fc), lambda i, c: (0, c)),
                  pl.BlockSpec((8, fc), lambda i, c: (0, c)),
                  pl.BlockSpec((fc, D_MODEL), lambda i, c: (c, 0)),
                  pl.BlockSpec((1, D_MODEL), lambda i, c: (0, 0))],
        out_specs=(pl.BlockSpec((tm, D_MODEL), lambda i, c: (i, 0)),
                   pl.BlockSpec((None, 8, fc), lambda i, c: (i, 0, c)),
                   pl.BlockSpec((None, 8, fc), lambda i, c: (i, 0, c))),
        scratch_shapes=[pltpu.VMEM((tm, D_MODEL), f32), pltpu.VMEM((2 * nch, 8, fc), f32)],
        compiler_params=_cparams(("arbitrary", "arbitrary")),
        name="conv_ffn_prompt",
    )(h, x1, wup, wup, cw, cw, first_a, first_g, wd, gf)


def _ffn_small_kernel(h_ref, x_ref, wa_ref, wg_ref, cwa_ref, cwg_ref, sta_ref, stg_ref, wd_ref, gf_ref,
                      y_ref, ua_ref, ug_ref, acc_ref, *, n_sample, n_tok):
    c = pl.program_id(1)
    rows = h_ref.shape[0]
    h = h_ref[...]
    row = lax.broadcasted_iota(jnp.int32, (rows, FF_CHUNK), 0)
    t = jnp.where(row < n_sample, row % n_tok, row - n_sample)

    def conv_half(w_ref, cw_ref, st_ref, u_ref):
        u = _dot(h, w_ref[...])
        u_ref[...] = u
        st = st_ref[...]
        u1 = jnp.where(t == 0, pltpu.roll(st, rows - 1, 0), pltpu.roll(u, 1, 0))
        u2 = jnp.where(t < 2, st, pltpu.roll(u, 2, 0))
        cw = cw_ref[...]
        return cw[2:3, :] * u + cw[1:2, :] * u1 + cw[0:1, :] * u2

    a = conv_half(wa_ref, cwa_ref, sta_ref, ua_ref)
    g = conv_half(wg_ref, cwg_ref, stg_ref, ug_ref)
    _ffn_finish(c, _gelu_tanh(a) * g, wd_ref, x_ref, gf_ref, y_ref, acc_ref)


def _ffn_small(h, x1, wup, cw, st, wd, gf, n_sample, n_tok):
    rows = h.shape[0]
    fc = FF_CHUNK
    nch = D_FF // fc
    return pl.pallas_call(
        functools.partial(_ffn_small_kernel, n_sample=n_sample, n_tok=n_tok),
        out_shape=(jax.ShapeDtypeStruct((rows, D_MODEL), f32),
                   jax.ShapeDtypeStruct((rows, D_FF), f32), jax.ShapeDtypeStruct((rows, D_FF), f32)),
        grid=(1, nch),
        in_specs=[pl.BlockSpec((rows, D_MODEL), lambda i, c: (0, 0)),
                  pl.BlockSpec((rows, D_MODEL), lambda i, c: (0, 0)),
                  pl.BlockSpec((D_MODEL, fc), lambda i, c: (0, c)),
                  pl.BlockSpec((D_MODEL, fc), lambda i, c: (0, nch + c)),
                  pl.BlockSpec((CONV_W, fc), lambda i, c: (0, c)),
                  pl.BlockSpec((CONV_W, fc), lambda i, c: (0, nch + c)),
                  pl.BlockSpec((rows, fc), lambda i, c: (0, c)),
                  pl.BlockSpec((rows, fc), lambda i, c: (0, nch + c)),
                  pl.BlockSpec((fc, D_MODEL), lambda i, c: (c, 0)),
                  pl.BlockSpec((1, D_MODEL), lambda i, c: (0, 0))],
        out_specs=(pl.BlockSpec((rows, D_MODEL), lambda i, c: (0, 0)),
                   pl.BlockSpec((rows, fc), lambda i, c: (0, c)),
                   pl.BlockSpec((rows, fc), lambda i, c: (0, c))),
        scratch_shapes=[pltpu.VMEM((rows, D_MODEL), f32)],
        compiler_params=_cparams(("arbitrary", "arbitrary")),
        name="conv_ffn_small",
    )(h, x1, wup, wup, cw, cw, st, st, wd, gf)


def _rope_tables(pos):
    half = RET_DK // 2
    inv = ROPE_BASE ** (-jnp.arange(half, dtype=f32) / half)
    ang = pos.astype(f32)[:, None] * inv[None, :]
    c, s = jnp.cos(ang), jnp.sin(ang)
    return jnp.concatenate([c, c], axis=-1), jnp.concatenate([-s, s], axis=-1)


def _layer(xp, xs, ck, cv, page_table, s_ret, s_conv, meta, norm1_g, w_in, sb_bias, w_ret_proj, w_sb_proj, w_o,
           norm2_g, w_up, conv_w, w_down, norm_f_g):
    batch, seq, d = xp.shape
    nb, n_tok, _ = xs.shape
    n_pool, page = ck.shape[0], ck.shape[1]
    n_pages = page_table.shape[1]
    past = n_pages * page
    n_sample = nb * n_tok
    n_small = n_sample + N_META

    w_in, w_ret_proj, w_sb_proj, w_o, w_up, w_down = (
        w.astype(bf16) for w in (w_in, w_ret_proj, w_sb_proj, w_o, w_up, w_down))
    g1, g2, gf = norm1_g[None, :], norm2_g[None, :], norm_f_g[None, :]
    x_big = xp.reshape(batch * seq, d)
    x_small = jnp.concatenate([xs.reshape(n_sample, d), meta.astype(f32)], axis=0)

    tm_big = 1024
    proj_s = _norm_mm(x_small, g1, w_in, n_small, 1024)
    proj_p = _norm_mm(x_big, g1, w_in, tm_big, 1024)

    lg = jnp.log1p(-jnp.exp2(-5.0 - jnp.arange(RET_HEADS, dtype=f32)))
    lg = jnp.broadcast_to(lg[:, None, None], (RET_HEADS, 1, RET_DV))
    cos_m, sin_m = _rope_tables(jnp.arange(N_META))
    ret_m, s_meta = _ret_small(proj_s, n_sample, N_META, N_META, cos_m, sin_m, lg,
                               jnp.zeros((1, RET_HEADS, RET_DK, RET_DV), f32))
    cos_p, sin_p = _rope_tables(N_META + jnp.arange(seq))
    ret_p, sret_p = _ret_prompt(proj_p, cos_p, sin_p, lg, s_meta[0], batch, seq)
    cos_s, sin_s = _rope_tables(past + (jnp.arange(SMALL_ROWS) % n_tok))
    ret_s, sret_s = _ret_small(proj_s, 0, n_sample, n_tok, cos_s, sin_s, lg, s_ret)

    tri = (jnp.arange(SB_BK)[:, None] > jnp.arange(SB_BK)[None, :]).astype(bf16)
    pad_rows = lambda a, n: jnp.pad(a, [(0, 0)] * (a.ndim - 2) + [(0, n - a.shape[-2]), (0, 0)])
    km = pad_rows(proj_s[n_sample:, C_SK:C_SK + SB_W], 128).astype(bf16)
    vm = pad_rows(proj_s[n_sample:, C_SV:C_SV + SB_W], 128).astype(bf16)
    sb_m = _sb_meta(proj_s, n_sample, km, vm, sb_bias, tri)
    sb_p = _sb_prompt(proj_p, km, vm, sb_bias, tri, batch, seq)
    ps3 = proj_s[:n_sample].reshape(nb, n_tok, IN_WIDTH)
    kn = pad_rows(ps3[:, :, C_SK:C_SK + SB_W], page).astype(bf16)
    vn = pad_rows(ps3[:, :, C_SV:C_SV + SB_W], page).astype(bf16)
    bias_col = jnp.tile(sb_bias, n_tok)[:, None]
    sb_s = _sb_sample(page_table, ps3[:, :, C_SQ:C_SQ + SB_W], kn, vn,
                      ck.reshape(n_pool, page, SB_W), cv.reshape(n_pool, page, SB_W), bias_col, tri)

    ret_small = jnp.concatenate([ret_s, ret_m], axis=0)
    sb_small = jnp.concatenate([sb_s.reshape(n_sample, SB_W).astype(bf16), sb_m], axis=0)
    m_s = _gate_mix(ret_small, sb_small, proj_s, w_ret_proj, w_sb_proj, n_small, 512)
    m_p = _gate_mix(ret_p, sb_p, proj_p, w_ret_proj, w_sb_proj, tm_big, 512)
    x1_s, h2_s = _out_proj(x_small, m_s, w_o, g2, n_small)
    x1_p, h2_p = _out_proj(x_big, m_p, w_o, g2, 512)

    st = jnp.concatenate([jnp.pad(s_conv, ((0, 0), (0, n_tok - (CONV_W - 1)), (0, 0))).reshape(n_sample, 2 * D_FF),
                          jnp.zeros((N_META, 2 * D_FF), f32)], axis=0)
    y_s, ua_s, ug_s = _ffn_small(h2_s, x1_s, w_up, conv_w, st, w_down, gf, n_sample, n_tok)
    y_p, ta_p, tg_p = _ffn_prompt(h2_p, x1_p, w_up, conv_w, ua_s[n_small - 8:], ug_s[n_small - 8:], w_down, gf, 512, seq)

    y_prompt = y_p.reshape(batch, seq, d)
    y_sample = y_s[:n_sample].reshape(nb, n_tok, d)

    def with_meta(col):
        real = proj_p[:, col:col + SB_W].reshape(batch, seq, SB_HEADS, SB_DH)
        m = jnp.broadcast_to(proj_s[n_sample:, col:col + SB_W].reshape(1, N_META, SB_HEADS, SB_DH),
                             (batch, N_META, SB_HEADS, SB_DH))
        return jnp.concatenate([m, real], axis=1)

    k_p, v_p = with_meta(C_SK), with_meta(C_SV)
    k_s = ps3[:, :, C_SK:C_SK + SB_W].reshape(nb, n_tok, SB_HEADS, SB_DH)
    v_s = ps3[:, :, C_SV:C_SV + SB_W].reshape(nb, n_tok, SB_HEADS, SB_DH)
    tiles_per_seq = seq // 512
    last = lambda tl: tl.reshape(batch, tiles_per_seq, 8, D_FF)[:, -1, 8 - (CONV_W - 1):, :]
    conv_p = jnp.concatenate([last(ta_p), last(tg_p)], axis=-1)
    u_s = jnp.concatenate([ua_s[:n_sample], ug_s[:n_sample]], axis=-1).reshape(nb, n_tok, 2 * D_FF)
    conv_s = u_s[:, n_tok - (CONV_W - 1):, :]
    return y_prompt, y_sample, k_p, v_p, k_s, v_s, sret_p, sret_s, conv_p, conv_s


def kernel(x_prompt, x_sample, cache_k, cache_v, page_table, state_ret, state_conv, meta_tokens, norm1_g, w_in,
           sb_bias, w_ret_proj, w_sb_proj, w_o, norm2_g, w_up, conv_w, w_down, norm_f_g):
    assert cache_k.shape[0] == 1, "one layer"
    outs = _layer(x_prompt, x_sample, cache_k[0], cache_v[0], page_table, state_ret[0], state_conv[0], meta_tokens,
                  norm1_g[0], w_in[0], sb_bias[0], w_ret_proj[0], w_sb_proj[0], w_o[0], norm2_g[0], w_up[0],
                  conv_w[0], w_down[0], norm_f_g)
    y_prompt, y_sample = outs[0], outs[1]
    return (y_prompt, y_sample) + tuple(o[None] for o in outs[2:])
```

```python
import functools

import jax
import jax.numpy as jnp
from jax import lax
from jax.experimental import pallas as pl
from jax.experimental.pallas import tpu as pltpu

f32 = jnp.float32
bf16 = jnp.bfloat16

D_MODEL = 2048
N_META = 16
RET_HEADS = 8
RET_DK = 128
RET_DV = 256
SB_HEADS = 16
SB_DH = 64
D_FF = 5632
CONV_W = 3
ROPE_BASE = 10000.0
EPS = 1e-6
GN_EPS = 1e-5

RET_QK = RET_HEADS * RET_DK
RET_V = RET_HEADS * RET_DV
SB_W = SB_HEADS * SB_DH
C_RQ, C_RK, C_RV, C_RG = 0, RET_QK, 2 * RET_QK, 2 * RET_QK + RET_V
C_SQ = C_RG + RET_V
C_SK, C_SV = C_SQ + SB_W, C_SQ + 2 * SB_W
C_GR = C_SV + SB_W
C_GS = C_GR + D_MODEL
IN_WIDTH = C_GS + D_MODEL

RET_CHUNK = 128
RET_HEAD_GROUP = 4
SB_BQ = 512
SB_BK = 256
SB_PAGES_PER_STEP = 8
LOG2E = 1.4426950408889634
SB_LOGIT_CLAMP = 100.0
SMALL_ROWS = 16
FF_CHUNK = 512
VMEM_LIMIT = 56 * 1024 * 1024


def _cparams(sem):
    return pltpu.CompilerParams(dimension_semantics=sem, vmem_limit_bytes=VMEM_LIMIT)


def _dot(a, b):
    return jnp.dot(a, b, preferred_element_type=f32)


def _dot_nt(a, b):
    return lax.dot_general(a, b, (((1,), (1,)), ((), ())), preferred_element_type=f32)


def _sigmoid(x):
    return 1.0 / (1.0 + jnp.exp(-x))


def _rmsnorm(x, g):
    return x * lax.rsqrt(jnp.mean(x * x, axis=-1, keepdims=True) + EPS) * g


def _norm_mm_kernel(x_ref, g_ref, w_ref, o_ref, h_ref):
    @pl.when(pl.program_id(1) == 0)
    def _():
        h_ref[...] = _rmsnorm(x_ref[...], g_ref[...]).astype(bf16)

    o_ref[...] = _dot(h_ref[...], w_ref[...])


def _norm_mm(x, g, w, tm, tn):
    m, d = x.shape
    n = w.shape[1]
    return pl.pallas_call(
        _norm_mm_kernel,
        out_shape=jax.ShapeDtypeStruct((m, n), f32),
        grid=(m // tm, n // tn),
        in_specs=[pl.BlockSpec((tm, d), lambda i, j: (i, 0)),
                  pl.BlockSpec((1, d), lambda i, j: (0, 0)),
                  pl.BlockSpec((d, tn), lambda i, j: (0, j))],
        out_specs=pl.BlockSpec((tm, tn), lambda i, j: (i, j)),
        scratch_shapes=[pltpu.VMEM((tm, d), bf16)],
        compiler_params=_cparams(("parallel", "arbitrary")),
        name="norm_in_proj",
    )(x, g, w)


def _rope(x, cos2, sin2):
    return x * cos2 + pltpu.roll(x, RET_DK // 2, 1) * sin2


def _gated_groupnorm(o, gate):
    mu = jnp.mean(o, axis=-1, keepdims=True)
    oc = o - mu
    var = jnp.mean(oc * oc, axis=-1, keepdims=True)
    return (gate * _sigmoid(gate)) * (oc * lax.rsqrt(var + GN_EPS))


def _ret_prompt_kernel(q_ref, k_ref, v_ref, g_ref, cos_ref, sin_ref, lg_ref, s0_ref, o_ref, sfin_ref, s_ref):
    c = pl.program_id(2)
    n = RET_CHUNK

    @pl.when(c == 0)
    def _():
        s_ref[...] = s0_ref[...]

    row = lax.broadcasted_iota(jnp.int32, (n, RET_DK), 0).astype(f32)
    col = lax.broadcasted_iota(jnp.int32, (n, n), 1).astype(f32)
    diff = row - col
    cos2, sin2 = cos_ref[...], sin_ref[...]
    for h in range(RET_HEAD_GROUP):
        dk = slice(h * RET_DK, (h + 1) * RET_DK)
        dv = slice(h * RET_DV, (h + 1) * RET_DV)
        lgv = lg_ref[h]
        lg = lgv[:, :RET_DK]
        q = _rope(q_ref[:, dk], cos2, sin2)
        k = _rope(k_ref[:, dk], cos2, sin2) * (RET_DK ** -0.5)
        decay = jnp.where(diff >= 0, jnp.exp(diff * lg), 0.0)
        att = _dot_nt(q.astype(bf16), k.astype(bf16)) * decay
        vb = v_ref[:, dv].astype(bf16)
        s = s_ref[h]
        o = _dot(att.astype(bf16), vb) + _dot((q * jnp.exp((row + 1.0) * lg)).astype(bf16), s.astype(bf16))
        kd = k * jnp.exp((n - 1.0 - row) * lg)
        s_ref[h] = jnp.exp(n * lgv) * s + _dot(kd.T.astype(bf16), vb)
        o_ref[:, dv] = _gated_groupnorm(o, g_ref[:, dv]).astype(bf16)

    @pl.when(c == pl.num_programs(2) - 1)
    def _():
        sfin_ref[...] = s_ref[...]


def _ret_prompt(proj, cos2, sin2, lg, s0, batch, seq):
    nc = seq // RET_CHUNK
    n = RET_CHUNK
    hg = RET_HEAD_GROUP
    wk, wv = hg * RET_DK, hg * RET_DV
    row = lambda b, h, c: b * nc + c
    return pl.pallas_call(
        _ret_prompt_kernel,
        out_shape=(jax.ShapeDtypeStruct((batch * seq, RET_V), bf16),
                   jax.ShapeDtypeStruct((batch, RET_HEADS, RET_DK, RET_DV), f32)),
        grid=(batch, RET_HEADS // hg, nc),
        in_specs=[pl.BlockSpec((n, wk), lambda b, h, c: (row(b, h, c), C_RQ // wk + h)),
                  pl.BlockSpec((n, wk), lambda b, h, c: (row(b, h, c), C_RK // wk + h)),
                  pl.BlockSpec((n, wv), lambda b, h, c: (row(b, h, c), C_RV // wv + h)),
                  pl.BlockSpec((n, wv), lambda b, h, c: (row(b, h, c), C_RG // wv + h)),
                  pl.BlockSpec((n, RET_DK), lambda b, h, c: (c, 0)),
                  pl.BlockSpec((n, RET_DK), lambda b, h, c: (c, 0)),
                  pl.BlockSpec((hg, 1, RET_DV), lambda b, h, c: (h, 0, 0)),
                  pl.BlockSpec((hg, RET_DK, RET_DV), lambda b, h, c: (h, 0, 0))],
        out_specs=(pl.BlockSpec((n, wv), lambda b, h, c: (row(b, h, c), h)),
                   pl.BlockSpec((None, hg, RET_DK, RET_DV), lambda b, h, c: (b, h, 0, 0))),
        scratch_shapes=[pltpu.VMEM((hg, RET_DK, RET_DV), f32)],
        compiler_params=_cparams(("parallel", "parallel", "arbitrary")),
        name="retention_prompt",
    )(proj, proj, proj, proj, cos2, sin2, lg, s0)


def _ret_small_kernel(q_ref, k_ref, v_ref, g_ref, cos_ref, sin_ref, lg_ref, s0_ref, o_ref, s_ref, *, n_tok):
    r = SMALL_ROWS
    n_seq = r // n_tok
    ri = lax.broadcasted_iota(jnp.int32, (r, RET_DK), 0)
    ti = (ri % n_tok).astype(f32)
    rj = lax.broadcasted_iota(jnp.int32, (r, 128), 1)
    rr = lax.broadcasted_iota(jnp.int32, (r, 128), 0)
    tdiff = ((rr % n_tok) - (rj % n_tok)).astype(f32)
    visible = ((rr // n_tok) == (rj // n_tok)) & (tdiff >= 0)
    rk = lax.broadcasted_iota(jnp.int32, (128, RET_DK), 0)
    tk = (rk % n_tok).astype(f32)
    rv = lax.broadcasted_iota(jnp.int32, (r, RET_DV), 0)
    cos2, sin2 = cos_ref[...], sin_ref[...]
    pad = 128 - r
    for h in range(RET_HEADS):
        lgv = lg_ref[h]
        lg = lgv[:, :RET_DK]
        q = _rope(q_ref[:, h * RET_DK:(h + 1) * RET_DK], cos2, sin2)
        k = _rope(k_ref[:, h * RET_DK:(h + 1) * RET_DK], cos2, sin2) * (RET_DK ** -0.5)
        kpad = jnp.concatenate([k, jnp.zeros((pad, RET_DK), f32)], axis=0)
        vpad = jnp.concatenate([v_ref[:, h * RET_DV:(h + 1) * RET_DV], jnp.zeros((pad, RET_DV), f32)],
                               axis=0).astype(bf16)
        decay = jnp.where(visible, jnp.exp(tdiff * lg), 0.0)
        att = _dot_nt(q.astype(bf16), kpad.astype(bf16)) * decay
        o = _dot(att.astype(bf16), vpad)
        qd = (q * jnp.exp((ti + 1.0) * lg)).astype(bf16)
        kd = kpad * jnp.exp((n_tok - 1.0 - tk) * lg)
        for s_i in range(n_seq):
            s = s0_ref[s_i, h]
            mine_k = (rk // n_tok) == s_i
            mine_v = (rv // n_tok) == s_i
            o = o + jnp.where(mine_v, _dot(qd, s.astype(bf16)), 0.0)
            s_ref[s_i, h] = jnp.exp(n_tok * lgv) * s + _dot(jnp.where(mine_k, kd, 0.0).T.astype(bf16), vpad)
        o_ref[:, h * RET_DV:(h + 1) * RET_DV] = _gated_groupnorm(
            o, g_ref[:, h * RET_DV:(h + 1) * RET_DV]).astype(bf16)


def _ret_small(proj, row0, n_rows, n_tok, cos2, sin2, lg, s0):
    r = SMALL_ROWS
    n_seq = r // n_tok
    steps = n_rows // r
    b0 = row0 // r
    return pl.pallas_call(
        functools.partial(_ret_small_kernel, n_tok=n_tok),
        out_shape=(jax.ShapeDtypeStruct((n_rows, RET_V), bf16),
                   jax.ShapeDtypeStruct(s0.shape, f32)),
        grid=(steps,),
        in_specs=[pl.BlockSpec((r, RET_QK), lambda i: (b0 + i, C_RQ // RET_QK)),
                  pl.BlockSpec((r, RET_QK), lambda i: (b0 + i, C_RK // RET_QK)),
                  pl.BlockSpec((r, RET_V), lambda i: (b0 + i, C_RV // RET_V)),
                  pl.BlockSpec((r, RET_V), lambda i: (b0 + i, C_RG // RET_V)),
                  pl.BlockSpec((r, RET_DK), lambda i: (0, 0)),
                  pl.BlockSpec((r, RET_DK), lambda i: (0, 0)),
                  pl.BlockSpec((RET_HEADS, 1, RET_DV), lambda i: (0, 0, 0)),
                  pl.BlockSpec((n_seq, RET_HEADS, RET_DK, RET_DV), lambda i: (i, 0, 0, 0))],
        out_specs=(pl.BlockSpec((r, RET_V), lambda i: (i, 0)),
                   pl.BlockSpec((n_seq, RET_HEADS, RET_DK, RET_DV), lambda i: (i, 0, 0, 0))),
        compiler_params=_cparams(("parallel",)),
        name="retention_small",
    )(proj, proj, proj, proj, cos2, sin2, lg, s0)


def _sb_log2_keep(z2, mask):
    zc = jnp.minimum(z2, SB_LOGIT_CLAMP)
    lk = jnp.log(1.0 + jnp.exp2(zc)) * -LOG2E
    return zc, (lk if mask is None else jnp.where(mask, lk, 0.0))


def _sb_terms(z2, tri, mask):
    zc, lk = _sb_log2_keep(z2, mask)
    between = _dot(lk.astype(bf16), tri)
    w = jnp.exp2(lk + between + zc)
    if mask is not None:
        w = jnp.where(mask, w, 0.0)
    return w.astype(bf16), between[:, :1] + lk[:, :1]


def _sb_prompt_kernel(bias_ref, q_ref, k_ref, v_ref, km_ref, vm_ref, tri_ref, o_ref, acc_ref, r_ref):
    p, i = pl.program_id(1), pl.program_id(2)
    bq, bk = SB_BQ, SB_BK
    q = q_ref[...] * (SB_DH ** -0.5 * LOG2E)
    lane = lax.broadcasted_iota(jnp.int32, (bq, 2 * SB_DH), 1)
    first = lane < SB_DH
    qm = (jnp.where(first, q, 0.0).astype(bf16), jnp.where(first, 0.0, q).astype(bf16))
    bias = (bias_ref[2 * p] * LOG2E, bias_ref[2 * p + 1] * LOG2E)
    tri = tri_ref[...]

    def terms(kb, vb, mask, tri_b):
        out = []
        for hh in range(2):
            w, total = _sb_terms(_dot_nt(qm[hh], kb) + bias[hh], tri_b, mask)
            out.append((_dot(w, vb), total))
        return out

    def accumulate(res):
        for hh, (contrib, total) in enumerate(res):
            r = r_ref[hh]
            acc_ref[hh] += jnp.exp2(r) * contrib
            r_ref[hh] = r + total

    def kv(j):
        s0 = pl.multiple_of(j * bk, bk)
        return k_ref[pl.ds(s0, bk), :].astype(bf16), v_ref[pl.ds(s0, bk), :].astype(bf16)

    nkb = bq // bk
    kk = lax.broadcasted_iota(jnp.int32, (bq, bk), 1)
    qq = lax.broadcasted_iota(jnp.int32, (bq, bk), 0)
    for d in reversed(range(nkb)):
        res = terms(*kv(i * nkb + d), kk + d * bk < qq, tri)
        if d == nkb - 1:
            for hh, (contrib, total) in enumerate(res):
                acc_ref[hh] = contrib
                r_ref[hh] = total
        else:
            accumulate(res)

    def pair(t, carry):
        newer = terms(*kv(i * nkb - 1 - 2 * t), None, tri)
        older = terms(*kv(i * nkb - 2 - 2 * t), None, tri)
        accumulate(newer)
        accumulate(older)
        return carry

    lax.fori_loop(0, i * (nkb // 2), pair, 0)

    nm = km_ref.shape[0]
    km_valid = lax.broadcasted_iota(jnp.int32, (bq, nm), 1) < N_META
    accumulate(terms(km_ref[...], vm_ref[...], km_valid, tri[:nm, :nm]))
    o_ref[...] = jnp.where(first, acc_ref[0], acc_ref[1]).astype(bf16)


def _sb_prompt(proj, km, vm, bias, tri, batch, seq):
    nq = seq // SB_BQ
    hp = 2 * SB_DH
    return pl.pallas_call(
        _sb_prompt_kernel,
        out_shape=jax.ShapeDtypeStruct((batch * seq, SB_W), bf16),
        grid_spec=pltpu.PrefetchScalarGridSpec(
            num_scalar_prefetch=1,
            grid=(batch, SB_HEADS // 2, nq),
            in_specs=[pl.BlockSpec((SB_BQ, hp), lambda b, p, i, s: (b * nq + i, C_SQ // hp + p)),
                      pl.BlockSpec((seq, hp), lambda b, p, i, s: (b, C_SK // hp + p)),
                      pl.BlockSpec((seq, hp), lambda b, p, i, s: (b, C_SV // hp + p)),
                      pl.BlockSpec((km.shape[0], hp), lambda b, p, i, s: (0, p)),
                      pl.BlockSpec((km.shape[0], hp), lambda b, p, i, s: (0, p)),
                      pl.BlockSpec((SB_BK, SB_BK), lambda b, p, i, s: (0, 0))],
            out_specs=pl.BlockSpec((SB_BQ, hp), lambda b, p, i, s: (b * nq + i, p)),
            scratch_shapes=[pltpu.VMEM((2, SB_BQ, hp), f32), pltpu.VMEM((2, SB_BQ, 1), f32)]),
        compiler_params=_cparams(("parallel", "parallel", "arbitrary")),
        name="stick_breaking_prompt",
    )(bias, proj, proj, proj, km, vm, tri)


def _sb_meta_kernel(bias_ref, q_ref, km_ref, vm_ref, tri_ref, o_ref):
    p = pl.program_id(0)
    nq, nm = q_ref.shape[0], km_ref.shape[0]
    q = q_ref[...] * (SB_DH ** -0.5 * LOG2E)
    first = lax.broadcasted_iota(jnp.int32, (nq, 2 * SB_DH), 1) < SB_DH
    mask = lax.broadcasted_iota(jnp.int32, (nq, nm), 1) < lax.broadcasted_iota(jnp.int32, (nq, nm), 0)
    tri = tri_ref[...][:nm, :nm]
    kb, vb = km_ref[...], vm_ref[...]
    out = []
    for hh, qh in enumerate((jnp.where(first, q, 0.0), jnp.where(first, 0.0, q))):
        z2 = _dot_nt(qh.astype(bf16), kb) + bias_ref[2 * p + hh] * LOG2E
        w, _ = _sb_terms(z2, tri, mask)
        out.append(_dot(w, vb))
    o_ref[...] = jnp.where(first, out[0], out[1]).astype(bf16)


def _sb_meta(proj_small, row0, km, vm, bias, tri):
    hp = 2 * SB_DH
    nm = km.shape[0]
    return pl.pallas_call(
        _sb_meta_kernel,
        out_shape=jax.ShapeDtypeStruct((N_META, SB_W), bf16),
        grid_spec=pltpu.PrefetchScalarGridSpec(
            num_scalar_prefetch=1,
            grid=(SB_HEADS // 2,),
            in_specs=[pl.BlockSpec((N_META, hp), lambda p, s: (row0 // N_META, C_SQ // hp + p)),
                      pl.BlockSpec((nm, hp), lambda p, s: (0, p)),
                      pl.BlockSpec((nm, hp), lambda p, s: (0, p)),
                      pl.BlockSpec((SB_BK, SB_BK), lambda p, s: (0, 0))],
            out_specs=pl.BlockSpec((N_META, hp), lambda p, s: (0, p))),
        compiler_params=_cparams(("parallel",)),
        name="stick_breaking_meta",
    )(bias, proj_small, km, vm, tri)


def _sb_sample_kernel(pt_ref, q_ref, kn_ref, vn_ref, *rest, n_tok, n_grp):
    k_refs, v_refs = rest[:n_grp], rest[n_grp:2 * n_grp]
    bias_ref, tri_ref, o_ref, qbd_ref, acc_ref, r_ref = rest[2 * n_grp:]
    j = pl.program_id(1)
    page = tri_ref.shape[0]
    rows = n_tok * SB_HEADS
    head_of_col = lax.broadcasted_iota(jnp.int32, (SB_HEADS, SB_W), 1) // SB_DH
    own = head_of_col == lax.broadcasted_iota(jnp.int32, (SB_HEADS, SB_W), 0)

    def blocks(kts, vts, mask, init):
        n = len(kts)
        qbd = qbd_ref[...]
        zc, lk = zip(*[_sb_log2_keep(_dot(qbd, kt) + bias_ref[...], mask) for kt in kts])
        between = _dot(jnp.concatenate([l.astype(bf16) for l in lk], axis=0), tri_ref[...])
        r = jnp.zeros((rows, 1), f32) if init else r_ref[...]
        acc = None if init else acc_ref[...]
        for g in range(n):
            btw = between[g * rows:(g + 1) * rows]
            w = jnp.exp2(lk[g] + btw + r + zc[g])
            if mask is not None:
                w = jnp.where(mask, w, 0.0)
            r = r + btw[:, :1] + lk[g][:, :1]
            contrib = _dot_nt(w.astype(bf16), vts[g])
            acc = contrib if acc is None else acc + contrib
        r_ref[...] = r
        acc_ref[...] = acc

    @pl.when(j == 0)
    def _():
        q = q_ref[...] * (SB_DH ** -0.5 * LOG2E)
        for t in range(n_tok):
            qbd_ref[t * SB_HEADS:(t + 1) * SB_HEADS, :] = jnp.where(own, q[t:t + 1, :], 0.0).astype(bf16)
        kk = lax.broadcasted_iota(jnp.int32, (rows, page), 1)
        tq = lax.broadcasted_iota(jnp.int32, (rows, page), 0) // SB_HEADS
        blocks([kn_ref[...]], [vn_ref[...]], kk < tq, True)

    @pl.when(j > 0)
    def _():
        blocks([r_[...].astype(bf16) for r_ in k_refs], [r_[...].astype(bf16) for r_ in v_refs], None, False)

    @pl.when(j == pl.num_programs(1) - 1)
    def _():
        for t in range(n_tok):
            sel = jnp.where(own, acc_ref[t * SB_HEADS:(t + 1) * SB_HEADS, :], 0.0)
            o_ref[t:t + 1, :] = jnp.sum(sel, axis=0, keepdims=True)


def _sb_sample(page_table, q3, kn, vn, ck, cv, bias_col, tri):
    nb, n_tok, _ = q3.shape
    n_pages = page_table.shape[1]
    page = ck.shape[2]
    rows = n_tok * SB_HEADS
    n_grp = SB_PAGES_PER_STEP
    assert n_pages % n_grp == 0

    def page_spec(g):
        return pl.BlockSpec((None, SB_W, page),
                            lambda b, j, pt: (pt[b, n_pages - 1 - ((jnp.maximum(j, 1) - 1) * n_grp + g)], 0, 0))

    fixed = lambda shape: pl.BlockSpec(shape, lambda b, j, pt: (0,) * len(shape))
    per_seq = lambda shape: pl.BlockSpec((None,) + shape, lambda b, j, pt: (b, 0, 0))
    return pl.pallas_call(
        functools.partial(_sb_sample_kernel, n_tok=n_tok, n_grp=n_grp),
        out_shape=jax.ShapeDtypeStruct((nb, n_tok, SB_W), f32),
        grid_spec=pltpu.PrefetchScalarGridSpec(
            num_scalar_prefetch=1,
            grid=(nb, n_pages // n_grp + 1),
            in_specs=([per_seq((n_tok, SB_W)), per_seq((SB_W, page)), per_seq((SB_W, page))]
                      + [page_spec(g) for g in range(n_grp)] + [page_spec(g) for g in range(n_grp)]
                      + [fixed((rows, 1)), fixed((page, page))]),
            out_specs=per_seq((n_tok, SB_W)),
            scratch_shapes=[pltpu.VMEM((rows, SB_W), bf16), pltpu.VMEM((rows, SB_W), f32),
                            pltpu.VMEM((rows, 1), f32)]),
        compiler_params=_cparams(("parallel", "arbitrary")),
        name="stick_breaking_sample",
    )(page_table, q3, kn, vn, *([ck] * n_grp), *([cv] * n_grp), bias_col, tri)


def _gate_mix_kernel(ret_ref, sb_ref, gr_ref, gs_ref, wr_ref, ws_ref, o_ref):
    m = _sigmoid(gr_ref[...]) * _dot(ret_ref[...], wr_ref[...]) + _sigmoid(gs_ref[...]) * _dot(sb_ref[...], ws_ref[...])
    o_ref[...] = m.astype(bf16)


def _gate_mix(ret, sb, proj, wr, ws, tm, tn):
    m = ret.shape[0]
    return pl.pallas_call(
        _gate_mix_kernel,
        out_shape=jax.ShapeDtypeStruct((m, D_MODEL), bf16),
        grid=(m // tm, D_MODEL // tn),
        in_specs=[pl.BlockSpec((tm, RET_V), lambda i, j: (i, 0)),
                  pl.BlockSpec((tm, SB_W), lambda i, j: (i, 0)),
                  pl.BlockSpec((tm, tn), lambda i, j: (i, C_GR // tn + j)),
                  pl.BlockSpec((tm, tn), lambda i, j: (i, C_GS // tn + j)),
                  pl.BlockSpec((RET_V, tn), lambda i, j: (0, j)),
                  pl.BlockSpec((SB_W, tn), lambda i, j: (0, j))],
        out_specs=pl.BlockSpec((tm, tn), lambda i, j: (i, j)),
        compiler_params=_cparams(("parallel", "arbitrary")),
        name="gate_mix",
    )(ret, sb, proj, proj, wr, ws)


def _out_proj_kernel(x_ref, m_ref, wo_ref, g_ref, x1_ref, h_ref):
    x1 = x_ref[...] + _dot(m_ref[...], wo_ref[...])
    x1_ref[...] = x1
    h_ref[...] = _rmsnorm(x1, g_ref[...]).astype(bf16)


def _out_proj(x, m, wo, g, tm):
    rows = x.shape[0]
    return pl.pallas_call(
        _out_proj_kernel,
        out_shape=(jax.ShapeDtypeStruct((rows, D_MODEL), f32), jax.ShapeDtypeStruct((rows, D_MODEL), bf16)),
        grid=(rows // tm,),
        in_specs=[pl.BlockSpec((tm, D_MODEL), lambda i: (i, 0)),
                  pl.BlockSpec((tm, D_MODEL), lambda i: (i, 0)),
                  pl.BlockSpec((D_MODEL, D_MODEL), lambda i: (0, 0)),
                  pl.BlockSpec((1, D_MODEL), lambda i: (0, 0))],
        out_specs=(pl.BlockSpec((tm, D_MODEL), lambda i: (i, 0)), pl.BlockSpec((tm, D_MODEL), lambda i: (i, 0))),
        compiler_params=_cparams(("parallel",)),
        name="out_proj_norm",
    )(x, m, wo, g)


def _gelu_tanh(x):
    return 0.5 * x * (1.0 + jnp.tanh(0.7978845608028654 * (x + 0.044715 * (x * x * x))))


def _ffn_finish(c, act, wd_ref, x_ref, gf_ref, y_ref, acc_ref):
    contrib = _dot(act.astype(bf16), wd_ref[...])

    @pl.when(c == 0)
    def _():
        acc_ref[...] = contrib

    @pl.when(c > 0)
    def _():
        acc_ref[...] += contrib

    @pl.when(c == pl.num_programs(1) - 1)
    def _():
        y_ref[...] = _rmsnorm(x_ref[...] + acc_ref[...], gf_ref[...])


def _ffn_prompt_kernel(h_ref, x_ref, wa_ref, wg_ref, cwa_ref, cwg_ref, pa_ref, pg_ref, wd_ref, gf_ref, y_ref, ta_ref, tg_ref,
                       acc_ref, prev_ref, *, tiles_per_seq, n_chunks):
    i, c = pl.program_id(0), pl.program_id(1)
    tm = h_ref.shape[0]
    h = h_ref[...]
    seq_start = (i % tiles_per_seq) == 0
    row = lax.broadcasted_iota(jnp.int32, (tm, FF_CHUNK), 0)

    def conv_half(w_ref, cw_ref, first_ref, slot, tail_ref):
        u = _dot(h, w_ref[...])
        prev = jnp.where(seq_start, first_ref[...], prev_ref[slot])
        u1 = jnp.where(row == 0, prev[7:8, :], pltpu.roll(u, 1, 0))
        u2 = jnp.where(row == 0, prev[6:7, :], jnp.where(row == 1, prev[7:8, :], pltpu.roll(u, 2, 0)))
        cw = cw_ref[...]
        tail = u[tm - 8:, :]
        prev_ref[slot] = tail
        tail_ref[...] = tail
        return cw[2:3, :] * u + cw[1:2, :] * u1 + cw[0:1, :] * u2

    a = conv_half(wa_ref, cwa_ref, pa_ref, c, ta_ref)
    g = conv_half(wg_ref, cwg_ref, pg_ref, n_chunks + c, tg_ref)
    _ffn_finish(c, _gelu_tanh(a) * g, wd_ref, x_ref, gf_ref, y_ref, acc_ref)


def _ffn_prompt(h, x1, wup, cw, first_a, first_g, wd, gf, tm, rows_per_seq):
    rows = h.shape[0]
    fc = FF_CHUNK
    nch = D_FF // fc
    nt = rows // tm
    return pl.pallas_call(
        functools.partial(_ffn_prompt_kernel, tiles_per_seq=rows_per_seq // tm, n_chunks=nch),
        out_shape=(jax.ShapeDtypeStruct((rows, D_MODEL), f32),
                   jax.ShapeDtypeStruct((nt, 8, D_FF), f32), jax.ShapeDtypeStruct((nt, 8, D_FF), f32)),
        grid=(nt, nch),
        in_specs=[pl.BlockSpec((tm, D_MODEL), lambda i, c: (i, 0)),
                  pl.BlockSpec((tm, D_MODEL), lambda i, c: (i, 0)),
                  pl.BlockSpec((D_MODEL, fc), lambda i, c: (0, c)),
                  pl.BlockSpec((D_MODEL, fc), lambda i, c: (0, nch + c)),
                  pl.BlockSpec((CONV_W, fc), lambda i, c: (0, c)),
                  pl.BlockSpec((CONV_W, fc), lambda i, c: (0, nch + c)),
                  pl.BlockSpec((8, fc), lambda i, c: (0, c)),
                  pl.BlockSpec((8, fc), lambda i, c: (0, c)),
                  pl.BlockSpec((fc, D_MODEL), lambda i, c: (c, 0)),
                  pl.BlockSpec((1, D_MODEL), lambda i, c: (0, 0))],
        out_specs=(pl.BlockSpec((tm, D_MODEL), lambda i, c: (i, 0)),
                   pl.BlockSpec((None, 8, fc), lambda i, c: (i, 0, c)),
                   pl.BlockSpec((None, 8, fc), lambda i, c: (i, 0, c))),
        scratch_shapes=[pltpu.VMEM((tm, D_MODEL), f32), pltpu.VMEM((2 * nch, 8, fc), f32)],
        compiler_params=_cparams(("arbitrary", "arbitrary")),
        name="conv_ffn_prompt",
    )(h, x1, wup, wup, cw, cw, first_a, first_g, wd, gf)


def _ffn_small_kernel(h_ref, x_ref, wa_ref, wg_ref, cwa_ref, cwg_ref, sta_ref, stg_ref, wd_ref, gf_ref,
                      y_ref, ua_ref, ug_ref, acc_ref, *, n_sample, n_tok):
    c = pl.program_id(1)
    rows = h_ref.shape[0]
    h = h_ref[...]
    row = lax.broadcasted_iota(jnp.int32, (rows, FF_CHUNK), 0)
    t = jnp.where(row < n_sample, row % n_tok, row - n_sample)

    def conv_half(w_ref, cw_ref, st_ref, u_ref):
        u = _dot(h, w_ref[...])
        u_ref[...] = u
        st = st_ref[...]
        u1 = jnp.where(t == 0, pltpu.roll(st, rows - 1, 0), pltpu.roll(u, 1, 0))
        u2 = jnp.where(t < 2, st, pltpu.roll(u, 2, 0))
        cw = cw_ref[...]
        return cw[2:3, :] * u + cw[1:2, :] * u1 + cw[0:1, :] * u2

    a = conv_half(wa_ref, cwa_ref, sta_ref, ua_ref)
    g = conv_half(wg_ref, cwg_ref, stg_ref, ug_ref)
    _ffn_finish(c, _gelu_tanh(a) * g, wd_ref, x_ref, gf_ref, y_ref, acc_ref)


def _ffn_small(h, x1, wup, cw, st, wd, gf, n_sample, n_tok):
    rows = h.shape[0]
    fc = FF_CHUNK
    nch = D_FF // fc
    return pl.pallas_call(
        functools.partial(_ffn_small_kernel, n_sample=n_sample, n_tok=n_tok),
        out_shape=(jax.ShapeDtypeStruct((rows, D_MODEL), f32),
                   jax.ShapeDtypeStruct((rows, D_FF), f32), jax.ShapeDtypeStruct((rows, D_FF), f32)),
        grid=(1, nch),
        in_specs=[pl.BlockSpec((rows, D_MODEL), lambda i, c: (0, 0)),
                  pl.BlockSpec((rows, D_MODEL), lambda i, c: (0, 0)),
                  pl.BlockSpec((D_MODEL, fc), lambda i, c: (0, c)),
                  pl.BlockSpec((D_MODEL, fc), lambda i, c: (0, nch + c)),
                  pl.BlockSpec((CONV_W, fc), lambda i, c: (0, c)),
                  pl.BlockSpec((CONV_W, fc), lambda i, c: (0, nch + c)),
                  pl.BlockSpec((rows, fc), lambda i, c: (0, c)),
                  pl.BlockSpec((rows, fc), lambda i, c: (0, nch + c)),
                  pl.BlockSpec((fc, D_MODEL), lambda i, c: (c, 0)),
                  pl.BlockSpec((1, D_MODEL), lambda i, c: (0, 0))],
        out_specs=(pl.BlockSpec((rows, D_MODEL), lambda i, c: (0, 0)),
                   pl.BlockSpec((rows, fc), lambda i, c: (0, c)),
                   pl.BlockSpec((rows, fc), lambda i, c: (0, c))),
        scratch_shapes=[pltpu.VMEM((rows, D_MODEL), f32)],
        compiler_params=_cparams(("arbitrary", "arbitrary")),
        name="conv_ffn_small",
    )(h, x1, wup, wup, cw, cw, st, st, wd, gf)


def _rope_tables(pos):
    half = RET_DK // 2
    inv = ROPE_BASE ** (-jnp.arange(half, dtype=f32) / half)
    ang = pos.astype(f32)[:, None] * inv[None, :]
    c, s = jnp.cos(ang), jnp.sin(ang)
    return jnp.concatenate([c, c], axis=-1), jnp.concatenate([-s, s], axis=-1)


def _layer(xp, xs, ck, cv, page_table, s_ret, s_conv, meta, norm1_g, w_in, sb_bias, w_ret_proj, w_sb_proj, w_o,
           norm2_g, w_up, conv_w, w_down, norm_f_g):
    batch, seq, d = xp.shape
    nb, n_tok, _ = xs.shape
    n_pool, page = ck.shape[0], ck.shape[1]
    n_pages = page_table.shape[1]
    past = n_pages * page
    n_sample = nb * n_tok
    n_small = n_sample + N_META

    w_in, w_ret_proj, w_sb_proj, w_o, w_up, w_down = (
        w.astype(bf16) for w in (w_in, w_ret_proj, w_sb_proj, w_o, w_up, w_down))
    g1, g2, gf = norm1_g[None, :], norm2_g[None, :], norm_f_g[None, :]
    x_big = xp.reshape(batch * seq, d)
    x_small = jnp.concatenate([xs.reshape(n_sample, d), meta.astype(f32)], axis=0)

    tm_big = 1024
    proj_s = _norm_mm(x_small, g1, w_in, n_small, 1024)
    proj_p = _norm_mm(x_big, g1, w_in, tm_big, 1024)

    lg = jnp.log1p(-jnp.exp2(-5.0 - jnp.arange(RET_HEADS, dtype=f32)))
    lg = jnp.broadcast_to(lg[:, None, None], (RET_HEADS, 1, RET_DV))
    cos_m, sin_m = _rope_tables(jnp.arange(N_META))
    ret_m, s_meta = _ret_small(proj_s, n_sample, N_META, N_META, cos_m, sin_m, lg,
                               jnp.zeros((1, RET_HEADS, RET_DK, RET_DV), f32))
    cos_p, sin_p = _rope_tables(N_META + jnp.arange(seq))
    ret_p, sret_p = _ret_prompt(proj_p, cos_p, sin_p, lg, s_meta[0], batch, seq)
    cos_s, sin_s = _rope_tables(past + (jnp.arange(SMALL_ROWS) % n_tok))
    ret_s, sret_s = _ret_small(proj_s, 0, n_sample, n_tok, cos_s, sin_s, lg, s_ret)

    tri = (jnp.arange(SB_BK)[:, None] > jnp.arange(SB_BK)[None, :]).astype(bf16)
    pad_rows = lambda a, n: jnp.pad(a, [(0, 0)] * (a.ndim - 2) + [(0, n - a.shape[-2]), (0, 0)])
    km = pad_rows(proj_s[n_sample:, C_SK:C_SK + SB_W], 128).astype(bf16)
    vm = pad_rows(proj_s[n_sample:, C_SV:C_SV + SB_W], 128).astype(bf16)
    sb_m = _sb_meta(proj_s, n_sample, km, vm, sb_bias, tri)
    sb_p = _sb_prompt(proj_p, km, vm, sb_bias, tri, batch, seq)
    ps3 = proj_s[:n_sample].reshape(nb, n_tok, IN_WIDTH)
    new_t = lambda col: jnp.pad(jnp.swapaxes(ps3[:, :, col:col + SB_W], 1, 2).astype(bf16),
                                ((0, 0), (0, 0), (0, page - n_tok)))
    cache_t = lambda c: jnp.transpose(c, (0, 2, 3, 1)).reshape(n_pool, SB_W, page)
    bias_col = jnp.tile(sb_bias * LOG2E, n_tok)[:, None]
    sb_s = _sb_sample(page_table, ps3[:, :, C_SQ:C_SQ + SB_W], new_t(C_SK), new_t(C_SV),
                      cache_t(ck), cache_t(cv), bias_col, tri[:page, :page])

    ret_small = jnp.concatenate([ret_s, ret_m], axis=0)
    sb_small = jnp.concatenate([sb_s.reshape(n_sample, SB_W).astype(bf16), sb_m], axis=0)
    m_s = _gate_mix(ret_small, sb_small, proj_s, w_ret_proj, w_sb_proj, n_small, 512)
    m_p = _gate_mix(ret_p, sb_p, proj_p, w_ret_proj, w_sb_proj, tm_big, 512)
    x1_s, h2_s = _out_proj(x_small, m_s, w_o, g2, n_small)
    x1_p, h2_p = _out_proj(x_big, m_p, w_o, g2, 512)

    st = jnp.concatenate([jnp.pad(s_conv, ((0, 0), (0, n_tok - (CONV_W - 1)), (0, 0))).reshape(n_sample, 2 * D_FF),
                          jnp.zeros((N_META, 2 * D_FF), f32)], axis=0)
    y_s, ua_s, ug_s = _ffn_small(h2_s, x1_s, w_up, conv_w, st, w_down, gf, n_sample, n_tok)
    y_p, ta_p, tg_p = _ffn_prompt(h2_p, x1_p, w_up, conv_w, ua_s[n_small - 8:], ug_s[n_small - 8:], w_down, gf, 512, seq)

    y_prompt = y_p.reshape(batch, seq, d)
    y_sample = y_s[:n_sample].reshape(nb, n_tok, d)

    def with_meta(col):
        real = proj_p[:, col:col + SB_W].reshape(batch, seq, SB_HEADS, SB_DH)
        m = jnp.broadcast_to(proj_s[n_sample:, col:col + SB_W].reshape(1, N_META, SB_HEADS, SB_DH),
                             (batch, N_META, SB_HEADS, SB_DH))
        return jnp.concatenate([m, real], axis=1)

    k_p, v_p = with_meta(C_SK), with_meta(C_SV)
    k_s = ps3[:, :, C_SK:C_SK + SB_W].reshape(nb, n_tok, SB_HEADS, SB_DH)
    v_s = ps3[:, :, C_SV:C_SV + SB_W].reshape(nb, n_tok, SB_HEADS, SB_DH)
    tiles_per_seq = seq // 512
    last = lambda tl: tl.reshape(batch, tiles_per_seq, 8, D_FF)[:, -1, 8 - (CONV_W - 1):, :]
    conv_p = jnp.concatenate([last(ta_p), last(tg_p)], axis=-1)
    u_s = jnp.concatenate([ua_s[:n_sample], ug_s[:n_sample]], axis=-1).reshape(nb, n_tok, 2 * D_FF)
    conv_s = u_s[:, n_tok - (CONV_W - 1):, :]
    return y_prompt, y_sample, k_p, v_p, k_s, v_s, sret_p, sret_s, conv_p, conv_s


def kernel(x_prompt, x_sample, cache_k, cache_v, page_table, state_ret, state_conv, meta_tokens, norm1_g, w_in,
           sb_bias, w_ret_proj, w_sb_proj, w_o, norm2_g, w_up, conv_w, w_down, norm_f_g):
    assert cache_k.shape[0] == 1, "one layer"
    outs = _layer(x_prompt, x_sample, cache_k[0], cache_v[0], page_table, state_ret[0], state_conv[0], meta_tokens,
                  norm1_g[0], w_in[0], sb_bias[0], w_ret_proj[0], w_sb_proj[0], w_o[0], norm2_g[0], w_up[0],
                  conv_w[0], w_down[0], norm_f_g)
    y_prompt, y_sample = outs[0], outs[1]
    return (y_prompt, y_sample) + tuple(o[None] for o in outs[2:])
```

```python
import functools

import jax
import jax.numpy as jnp
from jax import lax
from jax.experimental import pallas as pl
from jax.experimental.pallas import tpu as pltpu

f32 = jnp.float32
bf16 = jnp.bfloat16

D_MODEL = 2048
N_META = 16
RET_HEADS = 8
RET_DK = 128
RET_DV = 256
SB_HEADS = 16
SB_DH = 64
D_FF = 5632
CONV_W = 3
ROPE_BASE = 10000.0
EPS = 1e-6
GN_EPS = 1e-5

RET_QK = RET_HEADS * RET_DK
RET_V = RET_HEADS * RET_DV
SB_W = SB_HEADS * SB_DH
C_RQ, C_RK, C_RV, C_RG = 0, RET_QK, 2 * RET_QK, 2 * RET_QK + RET_V
C_SQ = C_RG + RET_V
C_SK, C_SV = C_SQ + SB_W, C_SQ + 2 * SB_W
C_GR = C_SV + SB_W
C_GS = C_GR + D_MODEL
IN_WIDTH = C_GS + D_MODEL

RET_CHUNK = 128
RET_HEAD_GROUP = 4
SB_BQ = 512
SB_BK = 256
SB_PAGES_PER_STEP = 8
LOG2E = 1.4426950408889634
SB_LOGIT_CLAMP = 100.0
SMALL_ROWS = 16
FF_CHUNK = 512
VMEM_LIMIT = 56 * 1024 * 1024


def _cparams(sem):
    return pltpu.CompilerParams(dimension_semantics=sem, vmem_limit_bytes=VMEM_LIMIT)


def _dot(a, b):
    return jnp.dot(a, b, preferred_element_type=f32)


def _dot_nt(a, b):
    return lax.dot_general(a, b, (((1,), (1,)), ((), ())), preferred_element_type=f32)


def _sigmoid(x):
    return 1.0 / (1.0 + jnp.exp(-x))


def _rmsnorm(x, g):
    return x * lax.rsqrt(jnp.mean(x * x, axis=-1, keepdims=True) + EPS) * g


def _norm_mm_kernel(x_ref, g_ref, w_ref, o_ref, h_ref):
    @pl.when(pl.program_id(1) == 0)
    def _():
        h_ref[...] = _rmsnorm(x_ref[...], g_ref[...]).astype(bf16)

    o_ref[...] = _dot(h_ref[...], w_ref[...])


def _norm_mm(x, g, w, tm, tn):
    m, d = x.shape
    n = w.shape[1]
    return pl.pallas_call(
        _norm_mm_kernel,
        out_shape=jax.ShapeDtypeStruct((m, n), f32),
        grid=(m // tm, n // tn),
        in_specs=[pl.BlockSpec((tm, d), lambda i, j: (i, 0)),
                  pl.BlockSpec((1, d), lambda i, j: (0, 0)),
                  pl.BlockSpec((d, tn), lambda i, j: (0, j))],
        out_specs=pl.BlockSpec((tm, tn), lambda i, j: (i, j)),
        scratch_shapes=[pltpu.VMEM((tm, d), bf16)],
        compiler_params=_cparams(("parallel", "arbitrary")),
        name="norm_in_proj",
    )(x, g, w)


def _rope(x, cos2, sin2):
    return x * cos2 + pltpu.roll(x, RET_DK // 2, 1) * sin2


def _gated_groupnorm(o, gate):
    mu = jnp.mean(o, axis=-1, keepdims=True)
    oc = o - mu
    var = jnp.mean(oc * oc, axis=-1, keepdims=True)
    return (gate * _sigmoid(gate)) * (oc * lax.rsqrt(var + GN_EPS))


def _ret_prompt_kernel(q_ref, k_ref, v_ref, g_ref, cos_ref, sin_ref, lg_ref, s0_ref, o_ref, sfin_ref, s_ref):
    c = pl.program_id(2)
    n = RET_CHUNK

    @pl.when(c == 0)
    def _():
        s_ref[...] = s0_ref[...]

    row = lax.broadcasted_iota(jnp.int32, (n, RET_DK), 0).astype(f32)
    col = lax.broadcasted_iota(jnp.int32, (n, n), 1).astype(f32)
    diff = row - col
    cos2, sin2 = cos_ref[...], sin_ref[...]
    for h in range(RET_HEAD_GROUP):
        dk = slice(h * RET_DK, (h + 1) * RET_DK)
        dv = slice(h * RET_DV, (h + 1) * RET_DV)
        lgv = lg_ref[h]
        lg = lgv[:, :RET_DK]
        q = _rope(q_ref[:, dk], cos2, sin2)
        k = _rope(k_ref[:, dk], cos2, sin2) * (RET_DK ** -0.5)
        decay = jnp.where(diff >= 0, jnp.exp(diff * lg), 0.0)
        att = _dot_nt(q.astype(bf16), k.astype(bf16)) * decay
        vb = v_ref[:, dv].astype(bf16)
        s = s_ref[h]
        o = _dot(att.astype(bf16), vb) + _dot((q * jnp.exp((row + 1.0) * lg)).astype(bf16), s.astype(bf16))
        kd = k * jnp.exp((n - 1.0 - row) * lg)
        s_ref[h] = jnp.exp(n * lgv) * s + _dot(kd.T.astype(bf16), vb)
        o_ref[:, dv] = _gated_groupnorm(o, g_ref[:, dv]).astype(bf16)

    @pl.when(c == pl.num_programs(2) - 1)
    def _():
        sfin_ref[...] = s_ref[...]


def _ret_prompt(proj, cos2, sin2, lg, s0, batch, seq):
    nc = seq // RET_CHUNK
    n = RET_CHUNK
    hg = RET_HEAD_GROUP
    wk, wv = hg * RET_DK, hg * RET_DV
    row = lambda b, h, c: b * nc + c
    return pl.pallas_call(
        _ret_prompt_kernel,
        out_shape=(jax.ShapeDtypeStruct((batch * seq, RET_V), bf16),
                   jax.ShapeDtypeStruct((batch, RET_HEADS, RET_DK, RET_DV), f32)),
        grid=(batch, RET_HEADS // hg, nc),
        in_specs=[pl.BlockSpec((n, wk), lambda b, h, c: (row(b, h, c), C_RQ // wk + h)),
                  pl.BlockSpec((n, wk), lambda b, h, c: (row(b, h, c), C_RK // wk + h)),
                  pl.BlockSpec((n, wv), lambda b, h, c: (row(b, h, c), C_RV // wv + h)),
                  pl.BlockSpec((n, wv), lambda b, h, c: (row(b, h, c), C_RG // wv + h)),
                  pl.BlockSpec((n, RET_DK), lambda b, h, c: (c, 0)),
                  pl.BlockSpec((n, RET_DK), lambda b, h, c: (c, 0)),
                  pl.BlockSpec((hg, 1, RET_DV), lambda b, h, c: (h, 0, 0)),
                  pl.BlockSpec((hg, RET_DK, RET_DV), lambda b, h, c: (h, 0, 0))],
        out_specs=(pl.BlockSpec((n, wv), lambda b, h, c: (row(b, h, c), h)),
                   pl.BlockSpec((None, hg, RET_DK, RET_DV), lambda b, h, c: (b, h, 0, 0))),
        scratch_shapes=[pltpu.VMEM((hg, RET_DK, RET_DV), f32)],
        compiler_params=_cparams(("parallel", "parallel", "arbitrary")),
        name="retention_prompt",
    )(proj, proj, proj, proj, cos2, sin2, lg, s0)


def _ret_small_kernel(q_ref, k_ref, v_ref, g_ref, cos_ref, sin_ref, lg_ref, s0_ref, o_ref, s_ref, *, n_tok):
    r = SMALL_ROWS
    n_seq = r // n_tok
    ri = lax.broadcasted_iota(jnp.int32, (r, RET_DK), 0)
    ti = (ri % n_tok).astype(f32)
    rj = lax.broadcasted_iota(jnp.int32, (r, 128), 1)
    rr = lax.broadcasted_iota(jnp.int32, (r, 128), 0)
    tdiff = ((rr % n_tok) - (rj % n_tok)).astype(f32)
    visible = ((rr // n_tok) == (rj // n_tok)) & (tdiff >= 0)
    rk = lax.broadcasted_iota(jnp.int32, (128, RET_DK), 0)
    tk = (rk % n_tok).astype(f32)
    rv = lax.broadcasted_iota(jnp.int32, (r, RET_DV), 0)
    cos2, sin2 = cos_ref[...], sin_ref[...]
    pad = 128 - r
    for h in range(RET_HEADS):
        lgv = lg_ref[h]
        lg = lgv[:, :RET_DK]
        q = _rope(q_ref[:, h * RET_DK:(h + 1) * RET_DK], cos2, sin2)
        k = _rope(k_ref[:, h * RET_DK:(h + 1) * RET_DK], cos2, sin2) * (RET_DK ** -0.5)
        kpad = jnp.concatenate([k, jnp.zeros((pad, RET_DK), f32)], axis=0)
        vpad = jnp.concatenate([v_ref[:, h * RET_DV:(h + 1) * RET_DV], jnp.zeros((pad, RET_DV), f32)],
                               axis=0).astype(bf16)
        decay = jnp.where(visible, jnp.exp(tdiff * lg), 0.0)
        att = _dot_nt(q.astype(bf16), kpad.astype(bf16)) * decay
        o = _dot(att.astype(bf16), vpad)
        qd = (q * jnp.exp((ti + 1.0) * lg)).astype(bf16)
        kd = kpad * jnp.exp((n_tok - 1.0 - tk) * lg)
        for s_i in range(n_seq):
            s = s0_ref[s_i, h]
            mine_k = (rk // n_tok) == s_i
            mine_v = (rv // n_tok) == s_i
            o = o + jnp.where(mine_v, _dot(qd, s.astype(bf16)), 0.0)
            s_ref[s_i, h] = jnp.exp(n_tok * lgv) * s + _dot(jnp.where(mine_k, kd, 0.0).T.astype(bf16), vpad)
        o_ref[:, h * RET_DV:(h + 1) * RET_DV] = _gated_groupnorm(
            o, g_ref[:, h * RET_DV:(h + 1) * RET_DV]).astype(bf16)


def _ret_small(proj, row0, n_rows, n_tok, cos2, sin2, lg, s0):
    r = SMALL_ROWS
    n_seq = r // n_tok
    steps = n_rows // r
    b0 = row0 // r
    return pl.pallas_call(
        functools.partial(_ret_small_kernel, n_tok=n_tok),
        out_shape=(jax.ShapeDtypeStruct((n_rows, RET_V), bf16),
                   jax.ShapeDtypeStruct(s0.shape, f32)),
        grid=(steps,),
        in_specs=[pl.BlockSpec((r, RET_QK), lambda i: (b0 + i, C_RQ // RET_QK)),
                  pl.BlockSpec((r, RET_QK), lambda i: (b0 + i, C_RK // RET_QK)),
                  pl.BlockSpec((r, RET_V), lambda i: (b0 + i, C_RV // RET_V)),
                  pl.BlockSpec((r, RET_V), lambda i: (b0 + i, C_RG // RET_V)),
                  pl.BlockSpec((r, RET_DK), lambda i: (0, 0)),
                  pl.BlockSpec((r, RET_DK), lambda i: (0, 0)),
                  pl.BlockSpec((RET_HEADS, 1, RET_DV), lambda i: (0, 0, 0)),
                  pl.BlockSpec((n_seq, RET_HEADS, RET_DK, RET_DV), lambda i: (i, 0, 0, 0))],
        out_specs=(pl.BlockSpec((r, RET_V), lambda i: (i, 0)),
                   pl.BlockSpec((n_seq, RET_HEADS, RET_DK, RET_DV), lambda i: (i, 0, 0, 0))),
        compiler_params=_cparams(("parallel",)),
        name="retention_small",
    )(proj, proj, proj, proj, cos2, sin2, lg, s0)


def _sb_log2_keep(z2, mask):
    zc = jnp.minimum(z2, SB_LOGIT_CLAMP)
    lk = jnp.log(1.0 + jnp.exp2(zc)) * -LOG2E
    return zc, (lk if mask is None else jnp.where(mask, lk, 0.0))


def _sb_terms(z2, tri, mask):
    zc, lk = _sb_log2_keep(z2, mask)
    between = _dot(lk.astype(bf16), tri)
    w = jnp.exp2(lk + between + zc)
    if mask is not None:
        w = jnp.where(mask, w, 0.0)
    return w.astype(bf16), between[:, :1] + lk[:, :1]


def _sb_prompt_kernel(bias_ref, q_ref, k_ref, v_ref, km_ref, vm_ref, tri_ref, o_ref, acc_ref, r_ref):
    p, i = pl.program_id(1), pl.program_id(2)
    bq, bk = SB_BQ, SB_BK
    q = q_ref[...] * (SB_DH ** -0.5 * LOG2E)
    lane = lax.broadcasted_iota(jnp.int32, (bq, 2 * SB_DH), 1)
    first = lane < SB_DH
    qm = (jnp.where(first, q, 0.0).astype(bf16), jnp.where(first, 0.0, q).astype(bf16))
    bias = (bias_ref[2 * p] * LOG2E, bias_ref[2 * p + 1] * LOG2E)
    tri = tri_ref[...]

    def terms(kb, vb, mask, tri_b):
        out = []
        for hh in range(2):
            w, total = _sb_terms(_dot_nt(qm[hh], kb) + bias[hh], tri_b, mask)
            out.append((_dot(w, vb), total))
        return out

    def accumulate(res):
        for hh, (contrib, total) in enumerate(res):
            r = r_ref[hh]
            acc_ref[hh] += jnp.exp2(r) * contrib
            r_ref[hh] = r + total

    def kv(j):
        s0 = pl.multiple_of(j * bk, bk)
        return k_ref[pl.ds(s0, bk), :].astype(bf16), v_ref[pl.ds(s0, bk), :].astype(bf16)

    nkb = bq // bk
    kk = lax.broadcasted_iota(jnp.int32, (bq, bk), 1)
    qq = lax.broadcasted_iota(jnp.int32, (bq, bk), 0)
    for d in reversed(range(nkb)):
        res = terms(*kv(i * nkb + d), kk + d * bk < qq, tri)
        if d == nkb - 1:
            for hh, (contrib, total) in enumerate(res):
                acc_ref[hh] = contrib
                r_ref[hh] = total
        else:
            accumulate(res)

    def pair(t, carry):
        newer = terms(*kv(i * nkb - 1 - 2 * t), None, tri)
        older = terms(*kv(i * nkb - 2 - 2 * t), None, tri)
        accumulate(newer)
        accumulate(older)
        return carry

    lax.fori_loop(0, i * (nkb // 2), pair, 0)

    nm = km_ref.shape[0]
    km_valid = lax.broadcasted_iota(jnp.int32, (bq, nm), 1) < N_META
    accumulate(terms(km_ref[...], vm_ref[...], km_valid, tri[:nm, :nm]))
    o_ref[...] = jnp.where(first, acc_ref[0], acc_ref[1]).astype(bf16)


def _sb_prompt(proj, km, vm, bias, tri, batch, seq):
    nq = seq // SB_BQ
    hp = 2 * SB_DH
    return pl.pallas_call(
        _sb_prompt_kernel,
        out_shape=jax.ShapeDtypeStruct((batch * seq, SB_W), bf16),
        grid_spec=pltpu.PrefetchScalarGridSpec(
            num_scalar_prefetch=1,
            grid=(batch, SB_HEADS // 2, nq),
            in_specs=[pl.BlockSpec((SB_BQ, hp), lambda b, p, i, s: (b * nq + i, C_SQ // hp + p)),
                      pl.BlockSpec((seq, hp), lambda b, p, i, s: (b, C_SK // hp + p)),
                      pl.BlockSpec((seq, hp), lambda b, p, i, s: (b, C_SV // hp + p)),
                      pl.BlockSpec((km.shape[0], hp), lambda b, p, i, s: (0, p)),
                      pl.BlockSpec((km.shape[0], hp), lambda b, p, i, s: (0, p)),
                      pl.BlockSpec((SB_BK, SB_BK), lambda b, p, i, s: (0, 0))],
            out_specs=pl.BlockSpec((SB_BQ, hp), lambda b, p, i, s: (b * nq + i, p)),
            scratch_shapes=[pltpu.VMEM((2, SB_BQ, hp), f32), pltpu.VMEM((2, SB_BQ, 1), f32)]),
        compiler_params=_cparams(("parallel", "parallel", "arbitrary")),
        name="stick_breaking_prompt",
    )(bias, proj, proj, proj, km, vm, tri)


def _sb_meta_kernel(bias_ref, q_ref, km_ref, vm_ref, tri_ref, o_ref):
    p = pl.program_id(0)
    nq, nm = q_ref.shape[0], km_ref.shape[0]
    q = q_ref[...] * (SB_DH ** -0.5 * LOG2E)
    first = lax.broadcasted_iota(jnp.int32, (nq, 2 * SB_DH), 1) < SB_DH
    mask = lax.broadcasted_iota(jnp.int32, (nq, nm), 1) < lax.broadcasted_iota(jnp.int32, (nq, nm), 0)
    tri = tri_ref[...][:nm, :nm]
    kb, vb = km_ref[...], vm_ref[...]
    out = []
    for hh, qh in enumerate((jnp.where(first, q, 0.0), jnp.where(first, 0.0, q))):
        z2 = _dot_nt(qh.astype(bf16), kb) + bias_ref[2 * p + hh] * LOG2E
        w, _ = _sb_terms(z2, tri, mask)
        out.append(_dot(w, vb))
    o_ref[...] = jnp.where(first, out[0], out[1]).astype(bf16)


def _sb_meta(proj_small, row0, km, vm, bias, tri):
    hp = 2 * SB_DH
    nm = km.shape[0]
    return pl.pallas_call(
        _sb_meta_kernel,
        out_shape=jax.ShapeDtypeStruct((N_META, SB_W), bf16),
        grid_spec=pltpu.PrefetchScalarGridSpec(
            num_scalar_prefetch=1,
            grid=(SB_HEADS // 2,),
            in_specs=[pl.BlockSpec((N_META, hp), lambda p, s: (row0 // N_META, C_SQ // hp + p)),
                      pl.BlockSpec((nm, hp), lambda p, s: (0, p)),
                      pl.BlockSpec((nm, hp), lambda p, s: (0, p)),
                      pl.BlockSpec((SB_BK, SB_BK), lambda p, s: (0, 0))],
            out_specs=pl.BlockSpec((N_META, hp), lambda p, s: (0, p))),
        compiler_params=_cparams(("parallel",)),
        name="stick_breaking_meta",
    )(bias, proj_small, km, vm, tri)


def _sb_sample_kernel(pt_ref, q_ref, kn_ref, vn_ref, *rest, n_tok, n_grp):
    k_refs, v_refs = rest[:n_grp], rest[n_grp:2 * n_grp]
    bias_ref, tri_ref, o_ref, qbd_ref, acc_ref, r_ref, new_ref = rest[2 * n_grp:]
    j = pl.program_id(1)
    page = tri_ref.shape[0]
    rows = n_tok * SB_HEADS
    head_of_col = lax.broadcasted_iota(jnp.int32, (SB_HEADS, SB_W), 1) // SB_DH
    own = head_of_col == lax.broadcasted_iota(jnp.int32, (SB_HEADS, SB_W), 0)

    def blocks(logits, weigh, mask, init):
        zc, lk = zip(*[_sb_log2_keep(z2 + bias_ref[...], mask) for z2 in logits])
        between = _dot(jnp.concatenate([l.astype(bf16) for l in lk], axis=0), tri_ref[...])
        r = jnp.zeros((rows, 1), f32) if init else r_ref[...]
        acc = None if init else acc_ref[...]
        for g in range(len(logits)):
            btw = between[g * rows:(g + 1) * rows]
            w = jnp.exp2(lk[g] + btw + r + zc[g])
            if mask is not None:
                w = jnp.where(mask, w, 0.0)
            r = r + btw[:, :1] + lk[g][:, :1]
            contrib = weigh[g](w.astype(bf16))
            acc = contrib if acc is None else acc + contrib
        r_ref[...] = r
        acc_ref[...] = acc

    @pl.when(j == 0)
    def _():
        q = q_ref[...] * (SB_DH ** -0.5 * LOG2E)
        for t in range(n_tok):
            qbd_ref[t * SB_HEADS:(t + 1) * SB_HEADS, :] = jnp.where(own, q[t:t + 1, :], 0.0).astype(bf16)
        new_ref[...] = jnp.zeros_like(new_ref)
        new_ref[0, :n_tok, :] = kn_ref[...]
        new_ref[1, :n_tok, :] = vn_ref[...]
        kb, vb = new_ref[0].astype(bf16), new_ref[1].astype(bf16)
        kk = lax.broadcasted_iota(jnp.int32, (rows, page), 1)
        tq = lax.broadcasted_iota(jnp.int32, (rows, page), 0) // SB_HEADS
        blocks([_dot_nt(qbd_ref[...], kb)], [lambda w: _dot(w, vb)], kk < tq, True)

    @pl.when(j > 0)
    def _():
        qbd = qbd_ref[...]
        blocks([_dot(qbd, r_[...].astype(bf16)) for r_ in k_refs],
               [lambda w, r_=r_: _dot_nt(w, r_[...].astype(bf16)) for r_ in v_refs], None, False)

    @pl.when(j == pl.num_programs(1) - 1)
    def _():
        for t in range(n_tok):
            sel = jnp.where(own, acc_ref[t * SB_HEADS:(t + 1) * SB_HEADS, :], 0.0)
            o_ref[t:t + 1, :] = jnp.sum(sel, axis=0, keepdims=True)


def _sb_sample(page_table, proj3, ck, cv, bias_col, tri):
    nb, n_tok, _ = proj3.shape
    n_pages = page_table.shape[1]
    page = ck.shape[2]
    rows = n_tok * SB_HEADS
    n_grp = SB_PAGES_PER_STEP
    assert n_pages % n_grp == 0

    def page_spec(g):
        return pl.BlockSpec((None, SB_W, page),
                            lambda b, j, pt: (pt[b, n_pages - 1 - ((jnp.maximum(j, 1) - 1) * n_grp + g)], 0, 0))

    fixed = lambda shape: pl.BlockSpec(shape, lambda b, j, pt: (0,) * len(shape))
    per_seq = lambda col: pl.BlockSpec((None, n_tok, SB_W), lambda b, j, pt: (b, 0, col // SB_W))
    return pl.pallas_call(
        functools.partial(_sb_sample_kernel, n_tok=n_tok, n_grp=n_grp),
        out_shape=jax.ShapeDtypeStruct((nb, n_tok, SB_W), f32),
        grid_spec=pltpu.PrefetchScalarGridSpec(
            num_scalar_prefetch=1,
            grid=(nb, n_pages // n_grp + 1),
            in_specs=([per_seq(C_SQ), per_seq(C_SK), per_seq(C_SV)]
                      + [page_spec(g) for g in range(n_grp)] + [page_spec(g) for g in range(n_grp)]
                      + [fixed((rows, 1)), fixed((page, page))]),
            out_specs=per_seq(0),
            scratch_shapes=[pltpu.VMEM((rows, SB_W), bf16), pltpu.VMEM((rows, SB_W), f32),
                            pltpu.VMEM((rows, 1), f32), pltpu.VMEM((2, page, SB_W), f32)]),
        compiler_params=_cparams(("parallel", "arbitrary")),
        name="stick_breaking_sample",
    )(page_table, proj3, proj3, proj3, *([ck] * n_grp), *([cv] * n_grp), bias_col, tri)


def _gate_mix_kernel(ret_ref, sb_ref, gr_ref, gs_ref, wr_ref, ws_ref, o_ref):
    m = _sigmoid(gr_ref[...]) * _dot(ret_ref[...], wr_ref[...]) + _sigmoid(gs_ref[...]) * _dot(sb_ref[...], ws_ref[...])
    o_ref[...] = m.astype(bf16)


def _gate_mix(ret, sb, proj, wr, ws, tm, tn):
    m = ret.shape[0]
    return pl.pallas_call(
        _gate_mix_kernel,
        out_shape=jax.ShapeDtypeStruct((m, D_MODEL), bf16),
        grid=(m // tm, D_MODEL // tn),
        in_specs=[pl.BlockSpec((tm, RET_V), lambda i, j: (i, 0)),
                  pl.BlockSpec((tm, SB_W), lambda i, j: (i, 0)),
                  pl.BlockSpec((tm, tn), lambda i, j: (i, C_GR // tn + j)),
                  pl.BlockSpec((tm, tn), lambda i, j: (i, C_GS // tn + j)),
                  pl.BlockSpec((RET_V, tn), lambda i, j: (0, j)),
                  pl.BlockSpec((SB_W, tn), lambda i, j: (0, j))],
        out_specs=pl.BlockSpec((tm, tn), lambda i, j: (i, j)),
        compiler_params=_cparams(("parallel", "arbitrary")),
        name="gate_mix",
    )(ret, sb, proj, proj, wr, ws)


def _out_proj_kernel(x_ref, m_ref, wo_ref, g_ref, x1_ref, h_ref):
    x1 = x_ref[...] + _dot(m_ref[...], wo_ref[...])
    x1_ref[...] = x1
    h_ref[...] = _rmsnorm(x1, g_ref[...]).astype(bf16)


def _out_proj(x, m, wo, g, tm):
    rows = x.shape[0]
    return pl.pallas_call(
        _out_proj_kernel,
        out_shape=(jax.ShapeDtypeStruct((rows, D_MODEL), f32), jax.ShapeDtypeStruct((rows, D_MODEL), bf16)),
        grid=(rows // tm,),
        in_specs=[pl.BlockSpec((tm, D_MODEL), lambda i: (i, 0)),
                  pl.BlockSpec((tm, D_MODEL), lambda i: (i, 0)),
                  pl.BlockSpec((D_MODEL, D_MODEL), lambda i: (0, 0)),
                  pl.BlockSpec((1, D_MODEL), lambda i: (0, 0))],
        out_specs=(pl.BlockSpec((tm, D_MODEL), lambda i: (i, 0)), pl.BlockSpec((tm, D_MODEL), lambda i: (i, 0))),
        compiler_params=_cparams(("parallel",)),
        name="out_proj_norm",
    )(x, m, wo, g)


def _gelu_tanh(x):
    return 0.5 * x * (1.0 + jnp.tanh(0.7978845608028654 * (x + 0.044715 * (x * x * x))))


def _ffn_finish(c, act, wd_ref, x_ref, gf_ref, y_ref, acc_ref):
    contrib = _dot(act.astype(bf16), wd_ref[...])

    @pl.when(c == 0)
    def _():
        acc_ref[...] = contrib

    @pl.when(c > 0)
    def _():
        acc_ref[...] += contrib

    @pl.when(c == pl.num_programs(1) - 1)
    def _():
        y_ref[...] = _rmsnorm(x_ref[...] + acc_ref[...], gf_ref[...])


def _ffn_prompt_kernel(h_ref, x_ref, wa_ref, wg_ref, cwa_ref, cwg_ref, pa_ref, pg_ref, wd_ref, gf_ref, y_ref, ta_ref, tg_ref,
                       acc_ref, prev_ref, act_ref, *, tiles_per_seq, n_chunks):
    i, c = pl.program_id(0), pl.program_id(1)
    tm = h_ref.shape[0]
    seq_start = (i % tiles_per_seq) == 0
    row = lax.broadcasted_iota(jnp.int32, (tm, FF_CHUNK), 0)

    def conv_half(h, w_ref, cw_ref, first_ref, slot, tail_ref):
        u = _dot(h, w_ref[...])
        prev = jnp.where(seq_start, first_ref[...], prev_ref[slot])
        u1 = jnp.where(row == 0, prev[7:8, :], pltpu.roll(u, 1, 0))
        u2 = jnp.where(row == 0, prev[6:7, :], jnp.where(row == 1, prev[7:8, :], pltpu.roll(u, 2, 0)))
        cw = cw_ref[...]
        tail = u[tm - 8:, :]
        prev_ref[slot] = tail
        tail_ref[...] = tail
        return cw[2:3, :] * u + cw[1:2, :] * u1 + cw[0:1, :] * u2

    def up():
        h = h_ref[...]
        a = conv_half(h, wa_ref, cwa_ref, pa_ref, c, ta_ref)
        g = conv_half(h, wg_ref, cwg_ref, pg_ref, n_chunks + c, tg_ref)
        act_ref[c % 2] = (_gelu_tanh(a) * g).astype(bf16)

    def down():
        return _dot(act_ref[(c + 1) % 2], wd_ref[...])

    @pl.when(c == 0)
    def _():
        acc_ref[...] = jnp.zeros_like(acc_ref)
        up()

    @pl.when((c > 0) & (c < n_chunks))
    def _():
        acc_ref[...] += down()
        up()

    @pl.when(c == n_chunks)
    def _():
        y_ref[...] = _rmsnorm(x_ref[...] + acc_ref[...] + down(), gf_ref[...])


def _ffn_prompt(h, x1, wup, cw, first_a, first_g, wd, gf, tm, rows_per_seq):
    rows = h.shape[0]
    fc = FF_CHUNK
    nch = D_FF // fc
    nt = rows // tm
    up_c = lambda c: jnp.minimum(c, nch - 1)
    return pl.pallas_call(
        functools.partial(_ffn_prompt_kernel, tiles_per_seq=rows_per_seq // tm, n_chunks=nch),
        out_shape=(jax.ShapeDtypeStruct((rows, D_MODEL), f32),
                   jax.ShapeDtypeStruct((nt, 8, D_FF), f32), jax.ShapeDtypeStruct((nt, 8, D_FF), f32)),
        grid=(nt, nch + 1),
        in_specs=[pl.BlockSpec((tm, D_MODEL), lambda i, c: (i, 0)),
                  pl.BlockSpec((tm, D_MODEL), lambda i, c: (i, 0)),
                  pl.BlockSpec((D_MODEL, fc), lambda i, c: (0, up_c(c))),
                  pl.BlockSpec((D_MODEL, fc), lambda i, c: (0, nch + up_c(c))),
                  pl.BlockSpec((CONV_W, fc), lambda i, c: (0, up_c(c))),
                  pl.BlockSpec((CONV_W, fc), lambda i, c: (0, nch + up_c(c))),
                  pl.BlockSpec((8, fc), lambda i, c: (0, up_c(c))),
                  pl.BlockSpec((8, fc), lambda i, c: (0, up_c(c))),
                  pl.BlockSpec((fc, D_MODEL), lambda i, c: (jnp.maximum(c - 1, 0), 0)),
                  pl.BlockSpec((1, D_MODEL), lambda i, c: (0, 0))],
        out_specs=(pl.BlockSpec((tm, D_MODEL), lambda i, c: (i, 0)),
                   pl.BlockSpec((None, 8, fc), lambda i, c: (i, 0, up_c(c))),
                   pl.BlockSpec((None, 8, fc), lambda i, c: (i, 0, up_c(c)))),
        scratch_shapes=[pltpu.VMEM((tm, D_MODEL), f32), pltpu.VMEM((2 * nch, 8, fc), f32),
                        pltpu.VMEM((2, tm, fc), bf16)],
        compiler_params=_cparams(("arbitrary", "arbitrary")),
        name="conv_ffn_prompt",
    )(h, x1, wup, wup, cw, cw, first_a, first_g, wd, gf)


def _ffn_small_kernel(h_ref, x_ref, wa_ref, wg_ref, cwa_ref, cwg_ref, sta_ref, stg_ref, wd_ref, gf_ref,
                      y_ref, ua_ref, ug_ref, acc_ref, *, n_sample, n_tok):
    c = pl.program_id(1)
    rows = h_ref.shape[0]
    h = h_ref[...]
    row = lax.broadcasted_iota(jnp.int32, (rows, FF_CHUNK), 0)
    t = jnp.where(row < n_sample, row % n_tok, row - n_sample)

    def conv_half(w_ref, cw_ref, st_ref, u_ref):
        u = _dot(h, w_ref[...])
        u_ref[...] = u
        st = st_ref[...]
        u1 = jnp.where(t == 0, pltpu.roll(st, rows - 1, 0), pltpu.roll(u, 1, 0))
        u2 = jnp.where(t < 2, st, pltpu.roll(u, 2, 0))
        cw = cw_ref[...]
        return cw[2:3, :] * u + cw[1:2, :] * u1 + cw[0:1, :] * u2

    a = conv_half(wa_ref, cwa_ref, sta_ref, ua_ref)
    g = conv_half(wg_ref, cwg_ref, stg_ref, ug_ref)
    _ffn_finish(c, _gelu_tanh(a) * g, wd_ref, x_ref, gf_ref, y_ref, acc_ref)


def _ffn_small(h, x1, wup, cw, st, wd, gf, n_sample, n_tok):
    rows = h.shape[0]
    fc = FF_CHUNK
    nch = D_FF // fc
    return pl.pallas_call(
        functools.partial(_ffn_small_kernel, n_sample=n_sample, n_tok=n_tok),
        out_shape=(jax.ShapeDtypeStruct((rows, D_MODEL), f32),
                   jax.ShapeDtypeStruct((rows, D_FF), f32), jax.ShapeDtypeStruct((rows, D_FF), f32)),
        grid=(1, nch),
        in_specs=[pl.BlockSpec((rows, D_MODEL), lambda i, c: (0, 0)),
                  pl.BlockSpec((rows, D_MODEL), lambda i, c: (0, 0)),
                  pl.BlockSpec((D_MODEL, fc), lambda i, c: (0, c)),
                  pl.BlockSpec((D_MODEL, fc), lambda i, c: (0, nch + c)),
                  pl.BlockSpec((CONV_W, fc), lambda i, c: (0, c)),
                  pl.BlockSpec((CONV_W, fc), lambda i, c: (0, nch + c)),
                  pl.BlockSpec((rows, fc), lambda i, c: (0, c)),
                  pl.BlockSpec((rows, fc), lambda i, c: (0, nch + c)),
                  pl.BlockSpec((fc, D_MODEL), lambda i, c: (c, 0)),
                  pl.BlockSpec((1, D_MODEL), lambda i, c: (0, 0))],
        out_specs=(pl.BlockSpec((rows, D_MODEL), lambda i, c: (0, 0)),
                   pl.BlockSpec((rows, fc), lambda i, c: (0, c)),
                   pl.BlockSpec((rows, fc), lambda i, c: (0, c))),
        scratch_shapes=[pltpu.VMEM((rows, D_MODEL), f32)],
        compiler_params=_cparams(("arbitrary", "arbitrary")),
        name="conv_ffn_small",
    )(h, x1, wup, wup, cw, cw, st, st, wd, gf)


def _rope_tables(pos):
    half = RET_DK // 2
    inv = ROPE_BASE ** (-jnp.arange(half, dtype=f32) / half)
    ang = pos.astype(f32)[:, None] * inv[None, :]
    c, s = jnp.cos(ang), jnp.sin(ang)
    return jnp.concatenate([c, c], axis=-1), jnp.concatenate([-s, s], axis=-1)


def _layer(xp, xs, ck, cv, page_table, s_ret, s_conv, meta, norm1_g, w_in, sb_bias, w_ret_proj, w_sb_proj, w_o,
           norm2_g, w_up, conv_w, w_down, norm_f_g):
    batch, seq, d = xp.shape
    nb, n_tok, _ = xs.shape
    n_pool, page = ck.shape[0], ck.shape[1]
    n_pages = page_table.shape[1]
    past = n_pages * page
    n_sample = nb * n_tok
    n_small = n_sample + N_META

    w_in, w_ret_proj, w_sb_proj, w_o, w_up, w_down = (
        w.astype(bf16) for w in (w_in, w_ret_proj, w_sb_proj, w_o, w_up, w_down))
    g1, g2, gf = norm1_g[None, :], norm2_g[None, :], norm_f_g[None, :]
    x_big = xp.reshape(batch * seq, d)
    x_small = jnp.concatenate([xs.reshape(n_sample, d), meta.astype(f32)], axis=0)

    tm_big = 1024
    proj_s = _norm_mm(x_small, g1, w_in, n_small, 1024)
    proj_p = _norm_mm(x_big, g1, w_in, tm_big, 1024)

    lg = jnp.log1p(-jnp.exp2(-5.0 - jnp.arange(RET_HEADS, dtype=f32)))
    lg = jnp.broadcast_to(lg[:, None, None], (RET_HEADS, 1, RET_DV))
    cos_m, sin_m = _rope_tables(jnp.arange(N_META))
    ret_m, s_meta = _ret_small(proj_s, n_sample, N_META, N_META, cos_m, sin_m, lg,
                               jnp.zeros((1, RET_HEADS, RET_DK, RET_DV), f32))
    cos_p, sin_p = _rope_tables(N_META + jnp.arange(seq))
    ret_p, sret_p = _ret_prompt(proj_p, cos_p, sin_p, lg, s_meta[0], batch, seq)
    cos_s, sin_s = _rope_tables(past + (jnp.arange(SMALL_ROWS) % n_tok))
    ret_s, sret_s = _ret_small(proj_s, 0, n_sample, n_tok, cos_s, sin_s, lg, s_ret)

    tri = (jnp.arange(SB_BK)[:, None] > jnp.arange(SB_BK)[None, :]).astype(bf16)
    pad_rows = lambda a, n: jnp.pad(a, [(0, 0)] * (a.ndim - 2) + [(0, n - a.shape[-2]), (0, 0)])
    km = pad_rows(proj_s[n_sample:, C_SK:C_SK + SB_W], 128).astype(bf16)
    vm = pad_rows(proj_s[n_sample:, C_SV:C_SV + SB_W], 128).astype(bf16)
    sb_m = _sb_meta(proj_s, n_sample, km, vm, sb_bias, tri)
    sb_p = _sb_prompt(proj_p, km, vm, sb_bias, tri, batch, seq)
    ps3 = proj_s[:n_sample].reshape(nb, n_tok, IN_WIDTH)
    cache_t = lambda c: jnp.transpose(c, (0, 2, 3, 1)).reshape(n_pool, SB_W, page)
    bias_col = jnp.tile(sb_bias * LOG2E, n_tok)[:, None]
    sb_s = _sb_sample(page_table, ps3, cache_t(ck), cache_t(cv), bias_col, tri[:page, :page])

    ret_small = jnp.concatenate([ret_s, ret_m], axis=0)
    sb_small = jnp.concatenate([sb_s.reshape(n_sample, SB_W).astype(bf16), sb_m], axis=0)
    m_s = _gate_mix(ret_small, sb_small, proj_s, w_ret_proj, w_sb_proj, n_small, 512)
    m_p = _gate_mix(ret_p, sb_p, proj_p, w_ret_proj, w_sb_proj, tm_big, 512)
    x1_s, h2_s = _out_proj(x_small, m_s, w_o, g2, n_small)
    x1_p, h2_p = _out_proj(x_big, m_p, w_o, g2, 512)

    st = jnp.concatenate([jnp.pad(s_conv, ((0, 0), (0, n_tok - (CONV_W - 1)), (0, 0))).reshape(n_sample, 2 * D_FF),
                          jnp.zeros((N_META, 2 * D_FF), f32)], axis=0)
    y_s, ua_s, ug_s = _ffn_small(h2_s, x1_s, w_up, conv_w, st, w_down, gf, n_sample, n_tok)
    y_p, ta_p, tg_p = _ffn_prompt(h2_p, x1_p, w_up, conv_w, ua_s[n_small - 8:], ug_s[n_small - 8:], w_down, gf, 512, seq)

    y_prompt = y_p.reshape(batch, seq, d)
    y_sample = y_s[:n_sample].reshape(nb, n_tok, d)

    def with_meta(col):
        real = proj_p[:, col:col + SB_W].reshape(batch, seq, SB_HEADS, SB_DH)
        m = jnp.broadcast_to(proj_s[n_sample:, col:col + SB_W].reshape(1, N_META, SB_HEADS, SB_DH),
                             (batch, N_META, SB_HEADS, SB_DH))
        return jnp.concatenate([m, real], axis=1)

    k_p, v_p = with_meta(C_SK), with_meta(C_SV)
    k_s = ps3[:, :, C_SK:C_SK + SB_W].reshape(nb, n_tok, SB_HEADS, SB_DH)
    v_s = ps3[:, :, C_SV:C_SV + SB_W].reshape(nb, n_tok, SB_HEADS, SB_DH)
    tiles_per_seq = seq // 512
    last = lambda tl: tl.reshape(batch, tiles_per_seq, 8, D_FF)[:, -1, 8 - (CONV_W - 1):, :]
    conv_p = jnp.concatenate([last(ta_p), last(tg_p)], axis=-1)
    u_s = jnp.concatenate([ua_s[:n_sample], ug_s[:n_sample]], axis=-1).reshape(nb, n_tok, 2 * D_FF)
    conv_s = u_s[:, n_tok - (CONV_W - 1):, :]
    return y_prompt, y_sample, k_p, v_p, k_s, v_s, sret_p, sret_s, conv_p, conv_s


def kernel(x_prompt, x_sample, cache_k, cache_v, page_table, state_ret, state_conv, meta_tokens, norm1_g, w_in,
           sb_bias, w_ret_proj, w_sb_proj, w_o, norm2_g, w_up, conv_w, w_down, norm_f_g):
    assert cache_k.shape[0] == 1, "one layer"
    outs = _layer(x_prompt, x_sample, cache_k[0], cache_v[0], page_table, state_ret[0], state_conv[0], meta_tokens,
                  norm1_g[0], w_in[0], sb_bias[0], w_ret_proj[0], w_sb_proj[0], w_o[0], norm2_g[0], w_up[0],
                  conv_w[0], w_down[0], norm_f_g)
    y_prompt, y_sample = outs[0], outs[1]
    return (y_prompt, y_sample) + tuple(o[None] for o in outs[2:])
```

```python
import functools

import jax
import jax.numpy as jnp
from jax import lax
from jax.experimental import pallas as pl
from jax.experimental.pallas import tpu as pltpu

f32 = jnp.float32
bf16 = jnp.bfloat16

D_MODEL = 2048
N_META = 16
RET_HEADS = 8
RET_DK = 128
RET_DV = 256
SB_HEADS = 16
SB_DH = 64
D_FF = 5632
CONV_W = 3
ROPE_BASE = 10000.0
EPS = 1e-6
GN_EPS = 1e-5

RET_QK = RET_HEADS * RET_DK
RET_V = RET_HEADS * RET_DV
SB_W = SB_HEADS * SB_DH
C_RQ, C_RK, C_RV, C_RG = 0, RET_QK, 2 * RET_QK, 2 * RET_QK + RET_V
C_SQ = C_RG + RET_V
C_SK, C_SV = C_SQ + SB_W, C_SQ + 2 * SB_W
C_GR = C_SV + SB_W
C_GS = C_GR + D_MODEL
IN_WIDTH = C_GS + D_MODEL

RET_CHUNK = 128
RET_HEAD_GROUP = 4
SB_BQ = 512
SB_BK = 256
LOG2E = 1.4426950408889634
SB_LOGIT_CLAMP = 100.0
SMALL_ROWS = 16
FF_CHUNK = 512
VMEM_LIMIT = 56 * 1024 * 1024


def _cparams(sem):
    return pltpu.CompilerParams(dimension_semantics=sem, vmem_limit_bytes=VMEM_LIMIT)


def _dot(a, b):
    return jnp.dot(a, b, preferred_element_type=f32)


def _dot_nt(a, b):
    return lax.dot_general(a, b, (((1,), (1,)), ((), ())), preferred_element_type=f32)


def _sigmoid(x):
    return 1.0 / (1.0 + jnp.exp(-x))


def _rmsnorm(x, g):
    return x * lax.rsqrt(jnp.mean(x * x, axis=-1, keepdims=True) + EPS) * g


def _norm_mm_kernel(x_ref, g_ref, w_ref, o_ref, h_ref):
    @pl.when(pl.program_id(1) == 0)
    def _():
        h_ref[...] = _rmsnorm(x_ref[...], g_ref[...]).astype(bf16)

    o_ref[...] = _dot(h_ref[...], w_ref[...])


def _norm_mm(x, g, w, tm, tn):
    m, d = x.shape
    n = w.shape[1]
    return pl.pallas_call(
        _norm_mm_kernel,
        out_shape=jax.ShapeDtypeStruct((m, n), f32),
        grid=(m // tm, n // tn),
        in_specs=[pl.BlockSpec((tm, d), lambda i, j: (i, 0)),
                  pl.BlockSpec((1, d), lambda i, j: (0, 0)),
                  pl.BlockSpec((d, tn), lambda i, j: (0, j))],
        out_specs=pl.BlockSpec((tm, tn), lambda i, j: (i, j)),
        scratch_shapes=[pltpu.VMEM((tm, d), bf16)],
        compiler_params=_cparams(("parallel", "arbitrary")),
        name="norm_in_proj",
    )(x, g, w)


def _rope(x, cos2, sin2):
    return x * cos2 + pltpu.roll(x, RET_DK // 2, 1) * sin2


def _gated_groupnorm(o, gate):
    mu = jnp.mean(o, axis=-1, keepdims=True)
    oc = o - mu
    var = jnp.mean(oc * oc, axis=-1, keepdims=True)
    return (gate * _sigmoid(gate)) * (oc * lax.rsqrt(var + GN_EPS))


def _ret_prompt_kernel(q_ref, k_ref, v_ref, g_ref, cos_ref, sin_ref, lg_ref, s0_ref, o_ref, sfin_ref, s_ref):
    c = pl.program_id(2)
    n = RET_CHUNK

    @pl.when(c == 0)
    def _():
        s_ref[...] = s0_ref[...]

    row = lax.broadcasted_iota(jnp.int32, (n, RET_DK), 0).astype(f32)
    col = lax.broadcasted_iota(jnp.int32, (n, n), 1).astype(f32)
    diff = row - col
    cos2, sin2 = cos_ref[...], sin_ref[...]
    for h in range(RET_HEAD_GROUP):
        dk = slice(h * RET_DK, (h + 1) * RET_DK)
        dv = slice(h * RET_DV, (h + 1) * RET_DV)
        lgv = lg_ref[h]
        lg = lgv[:, :RET_DK]
        q = _rope(q_ref[:, dk], cos2, sin2)
        k = _rope(k_ref[:, dk], cos2, sin2) * (RET_DK ** -0.5)
        decay = jnp.where(diff >= 0, jnp.exp(diff * lg), 0.0)
        att = _dot_nt(q.astype(bf16), k.astype(bf16)) * decay
        vb = v_ref[:, dv].astype(bf16)
        s = s_ref[h]
        o = _dot(att.astype(bf16), vb) + _dot((q * jnp.exp((row + 1.0) * lg)).astype(bf16), s.astype(bf16))
        kd = k * jnp.exp((n - 1.0 - row) * lg)
        s_ref[h] = jnp.exp(n * lgv) * s + _dot(kd.T.astype(bf16), vb)
        o_ref[:, dv] = _gated_groupnorm(o, g_ref[:, dv]).astype(bf16)

    @pl.when(c == pl.num_programs(2) - 1)
    def _():
        sfin_ref[...] = s_ref[...]


def _ret_prompt(proj, cos2, sin2, lg, s0, batch, seq):
    nc = seq // RET_CHUNK
    n = RET_CHUNK
    hg = RET_HEAD_GROUP
    wk, wv = hg * RET_DK, hg * RET_DV
    row = lambda b, h, c: b * nc + c
    return pl.pallas_call(
        _ret_prompt_kernel,
        out_shape=(jax.ShapeDtypeStruct((batch * seq, RET_V), bf16),
                   jax.ShapeDtypeStruct((batch, RET_HEADS, RET_DK, RET_DV), f32)),
        grid=(batch, RET_HEADS // hg, nc),
        in_specs=[pl.BlockSpec((n, wk), lambda b, h, c: (row(b, h, c), C_RQ // wk + h)),
                  pl.BlockSpec((n, wk), lambda b, h, c: (row(b, h, c), C_RK // wk + h)),
                  pl.BlockSpec((n, wv), lambda b, h, c: (row(b, h, c), C_RV // wv + h)),
                  pl.BlockSpec((n, wv), lambda b, h, c: (row(b, h, c), C_RG // wv + h)),
                  pl.BlockSpec((n, RET_DK), lambda b, h, c: (c, 0)),
                  pl.BlockSpec((n, RET_DK), lambda b, h, c: (c, 0)),
                  pl.BlockSpec((hg, 1, RET_DV), lambda b, h, c: (h, 0, 0)),
                  pl.BlockSpec((hg, RET_DK, RET_DV), lambda b, h, c: (h, 0, 0))],
        out_specs=(pl.BlockSpec((n, wv), lambda b, h, c: (row(b, h, c), h)),
                   pl.BlockSpec((None, hg, RET_DK, RET_DV), lambda b, h, c: (b, h, 0, 0))),
        scratch_shapes=[pltpu.VMEM((hg, RET_DK, RET_DV), f32)],
        compiler_params=_cparams(("parallel", "parallel", "arbitrary")),
        name="retention_prompt",
    )(proj, proj, proj, proj, cos2, sin2, lg, s0)


def _ret_small_kernel(q_ref, k_ref, v_ref, g_ref, cos_ref, sin_ref, lg_ref, s0_ref, o_ref, s_ref, *, n_tok):
    r = SMALL_ROWS
    n_seq = r // n_tok
    ri = lax.broadcasted_iota(jnp.int32, (r, RET_DK), 0)
    ti = (ri % n_tok).astype(f32)
    rj = lax.broadcasted_iota(jnp.int32, (r, 128), 1)
    rr = lax.broadcasted_iota(jnp.int32, (r, 128), 0)
    tdiff = ((rr % n_tok) - (rj % n_tok)).astype(f32)
    visible = ((rr // n_tok) == (rj // n_tok)) & (tdiff >= 0)
    rk = lax.broadcasted_iota(jnp.int32, (128, RET_DK), 0)
    tk = (rk % n_tok).astype(f32)
    rv = lax.broadcasted_iota(jnp.int32, (r, RET_DV), 0)
    cos2, sin2 = cos_ref[...], sin_ref[...]
    pad = 128 - r
    for h in range(RET_HEADS):
        lgv = lg_ref[h]
        lg = lgv[:, :RET_DK]
        q = _rope(q_ref[:, h * RET_DK:(h + 1) * RET_DK], cos2, sin2)
        k = _rope(k_ref[:, h * RET_DK:(h + 1) * RET_DK], cos2, sin2) * (RET_DK ** -0.5)
        kpad = jnp.concatenate([k, jnp.zeros((pad, RET_DK), f32)], axis=0)
        vpad = jnp.concatenate([v_ref[:, h * RET_DV:(h + 1) * RET_DV], jnp.zeros((pad, RET_DV), f32)],
                               axis=0).astype(bf16)
        decay = jnp.where(visible, jnp.exp(tdiff * lg), 0.0)
        att = _dot_nt(q.astype(bf16), kpad.astype(bf16)) * decay
        o = _dot(att.astype(bf16), vpad)
        qd = (q * jnp.exp((ti + 1.0) * lg)).astype(bf16)
        kd = kpad * jnp.exp((n_tok - 1.0 - tk) * lg)
        for s_i in range(n_seq):
            s = s0_ref[s_i, h]
            mine_k = (rk // n_tok) == s_i
            mine_v = (rv // n_tok) == s_i
            o = o + jnp.where(mine_v, _dot(qd, s.astype(bf16)), 0.0)
            s_ref[s_i, h] = jnp.exp(n_tok * lgv) * s + _dot(jnp.where(mine_k, kd, 0.0).T.astype(bf16), vpad)
        o_ref[:, h * RET_DV:(h + 1) * RET_DV] = _gated_groupnorm(
            o, g_ref[:, h * RET_DV:(h + 1) * RET_DV]).astype(bf16)


def _ret_small(proj, row0, n_rows, n_tok, cos2, sin2, lg, s0):
    r = SMALL_ROWS
    n_seq = r // n_tok
    steps = n_rows // r
    b0 = row0 // r
    return pl.pallas_call(
        functools.partial(_ret_small_kernel, n_tok=n_tok),
        out_shape=(jax.ShapeDtypeStruct((n_rows, RET_V), bf16),
                   jax.ShapeDtypeStruct(s0.shape, f32)),
        grid=(steps,),
        in_specs=[pl.BlockSpec((r, RET_QK), lambda i: (b0 + i, C_RQ // RET_QK)),
                  pl.BlockSpec((r, RET_QK), lambda i: (b0 + i, C_RK // RET_QK)),
                  pl.BlockSpec((r, RET_V), lambda i: (b0 + i, C_RV // RET_V)),
                  pl.BlockSpec((r, RET_V), lambda i: (b0 + i, C_RG // RET_V)),
                  pl.BlockSpec((r, RET_DK), lambda i: (0, 0)),
                  pl.BlockSpec((r, RET_DK), lambda i: (0, 0)),
                  pl.BlockSpec((RET_HEADS, 1, RET_DV), lambda i: (0, 0, 0)),
                  pl.BlockSpec((n_seq, RET_HEADS, RET_DK, RET_DV), lambda i: (i, 0, 0, 0))],
        out_specs=(pl.BlockSpec((r, RET_V), lambda i: (i, 0)),
                   pl.BlockSpec((n_seq, RET_HEADS, RET_DK, RET_DV), lambda i: (i, 0, 0, 0))),
        compiler_params=_cparams(("parallel",)),
        name="retention_small",
    )(proj, proj, proj, proj, cos2, sin2, lg, s0)


def _sb_log2_keep(z2, mask):
    zc = jnp.minimum(z2, SB_LOGIT_CLAMP)
    lk = jnp.log(1.0 + jnp.exp2(zc)) * -LOG2E
    return zc, (lk if mask is None else jnp.where(mask, lk, 0.0))


def _sb_terms(z2, tri, mask):
    zc, lk = _sb_log2_keep(z2, mask)
    between = _dot(lk.astype(bf16), tri)
    w = jnp.exp2(lk + between + zc)
    if mask is not None:
        w = jnp.where(mask, w, 0.0)
    return w.astype(bf16), between[:, :1] + lk[:, :1]


def _sb_kernel(pt_ref, bias_ref, q_ref, k_ref, v_ref, km_ref, vm_ref, tri_ref, qs_ref, kn_ref, vn_ref, *rest,
               n_tok, n_grp):
    k_refs, v_refs = rest[:n_grp], rest[n_grp:2 * n_grp]
    bcol_ref, o_ref, os_ref, acc_ref, r_ref, qbd_ref, accs_ref, rs_ref, new_ref = rest[2 * n_grp:]
    p, i = pl.program_id(1), pl.program_id(2)
    bq, bk = SB_BQ, SB_BK
    tri = tri_ref[...]

    page = k_refs[0].shape[1]
    rows = n_tok * SB_HEADS
    head_of_col = lax.broadcasted_iota(jnp.int32, (SB_HEADS, SB_W), 1) // SB_DH
    own = head_of_col == lax.broadcasted_iota(jnp.int32, (SB_HEADS, SB_W), 0)
    tri_p = tri[:page, :page]

    def sample_blocks(logits, weigh, mask, init):
        zc, lk = zip(*[_sb_log2_keep(z2 + bcol_ref[...], mask) for z2 in logits])
        between = _dot(jnp.concatenate([l.astype(bf16) for l in lk], axis=0), tri_p)
        r = jnp.zeros((rows, 1), f32) if init else rs_ref[...]
        acc = None if init else accs_ref[...]
        for g in range(len(logits)):
            btw = between[g * rows:(g + 1) * rows]
            w = jnp.exp2(lk[g] + btw + r + zc[g])
            if mask is not None:
                w = jnp.where(mask, w, 0.0)
            r = r + btw[:, :1] + lk[g][:, :1]
            contrib = weigh[g](w.astype(bf16))
            acc = contrib if acc is None else acc + contrib
        rs_ref[...] = r
        accs_ref[...] = acc

    @pl.when(i == 0)
    def _():
        qs = qs_ref[...] * (SB_DH ** -0.5 * LOG2E)
        for t in range(n_tok):
            qbd_ref[t * SB_HEADS:(t + 1) * SB_HEADS, :] = jnp.where(own, qs[t:t + 1, :], 0.0).astype(bf16)
        new_ref[...] = jnp.zeros_like(new_ref)
        new_ref[0, :n_tok, :] = kn_ref[...]
        new_ref[1, :n_tok, :] = vn_ref[...]
        kb, vb = new_ref[0].astype(bf16), new_ref[1].astype(bf16)
        kk = lax.broadcasted_iota(jnp.int32, (rows, page), 1)
        tq = lax.broadcasted_iota(jnp.int32, (rows, page), 0) // SB_HEADS
        sample_blocks([_dot_nt(qbd_ref[...], kb)], [lambda w: _dot(w, vb)], kk < tq, True)

    qbd = qbd_ref[...]
    sample_blocks([_dot(qbd, r_[...].astype(bf16)) for r_ in k_refs],
                  [lambda w, r_=r_: _dot_nt(w, r_[...].astype(bf16)) for r_ in v_refs], None, False)

    q = q_ref[...] * (SB_DH ** -0.5 * LOG2E)
    lane = lax.broadcasted_iota(jnp.int32, (bq, 2 * SB_DH), 1)
    first = lane < SB_DH
    qm = (jnp.where(first, q, 0.0).astype(bf16), jnp.where(first, 0.0, q).astype(bf16))
    bias = (bias_ref[2 * p] * LOG2E, bias_ref[2 * p + 1] * LOG2E)

    def terms(kb, vb, mask, tri_b):
        out = []
        for hh in range(2):
            w, total = _sb_terms(_dot_nt(qm[hh], kb) + bias[hh], tri_b, mask)
            out.append((_dot(w, vb), total))
        return out

    def accumulate(res):
        for hh, (contrib, total) in enumerate(res):
            r = r_ref[hh]
            acc_ref[hh] += jnp.exp2(r) * contrib
            r_ref[hh] = r + total

    def kv(j):
        s0 = pl.multiple_of(j * bk, bk)
        return k_ref[pl.ds(s0, bk), :].astype(bf16), v_ref[pl.ds(s0, bk), :].astype(bf16)

    nkb = bq // bk
    kk = lax.broadcasted_iota(jnp.int32, (bq, bk), 1)
    qq = lax.broadcasted_iota(jnp.int32, (bq, bk), 0)
    for d in reversed(range(nkb)):
        res = terms(*kv(i * nkb + d), kk + d * bk < qq, tri)
        if d == nkb - 1:
            for hh, (contrib, total) in enumerate(res):
                acc_ref[hh] = contrib
                r_ref[hh] = total
        else:
            accumulate(res)

    def pair(t, carry):
        newer = terms(*kv(i * nkb - 1 - 2 * t), None, tri)
        older = terms(*kv(i * nkb - 2 - 2 * t), None, tri)
        accumulate(newer)
        accumulate(older)
        return carry

    lax.fori_loop(0, i * (nkb // 2), pair, 0)

    nm = km_ref.shape[0]
    km_valid = lax.broadcasted_iota(jnp.int32, (bq, nm), 1) < N_META
    accumulate(terms(km_ref[...], vm_ref[...], km_valid, tri[:nm, :nm]))
    o_ref[...] = jnp.where(first, acc_ref[0], acc_ref[1]).astype(bf16)

    @pl.when(i == pl.num_programs(2) - 1)
    def _():
        for t in range(n_tok):
            sel = jnp.where(own, accs_ref[t * SB_HEADS:(t + 1) * SB_HEADS, :], 0.0)
            os_ref[t:t + 1, :] = jnp.sum(sel, axis=0, keepdims=True)


def _stick_breaking(proj, km, vm, bias, tri, batch, seq, page_table, proj3, ck, cv, bias_col):
    nq = seq // SB_BQ
    hp = 2 * SB_DH
    n_pairs = SB_HEADS // 2
    nb, n_tok, _ = proj3.shape
    n_pages = page_table.shape[1]
    page = ck.shape[2]
    rows = n_tok * SB_HEADS
    assert nb == batch * n_pairs and n_pages % nq == 0
    n_grp = n_pages // nq

    def page_spec(g):
        return pl.BlockSpec((None, SB_W, page),
                            lambda b, p, i, pt, bs: (pt[b * n_pairs + p, n_pages - 1 - (i * n_grp + g)], 0, 0))

    new_tok = lambda col: pl.BlockSpec((None, n_tok, SB_W), lambda b, p, i, pt, bs: (b * n_pairs + p, 0, col // SB_W))
    return pl.pallas_call(
        functools.partial(_sb_kernel, n_tok=n_tok, n_grp=n_grp),
        out_shape=(jax.ShapeDtypeStruct((batch * seq, SB_W), bf16), jax.ShapeDtypeStruct((nb, n_tok, SB_W), f32)),
        grid_spec=pltpu.PrefetchScalarGridSpec(
            num_scalar_prefetch=2,
            grid=(batch, n_pairs, nq),
            in_specs=([pl.BlockSpec((SB_BQ, hp), lambda b, p, i, pt, bs: (b * nq + i, C_SQ // hp + p)),
                       pl.BlockSpec((seq, hp), lambda b, p, i, pt, bs: (b, C_SK // hp + p)),
                       pl.BlockSpec((seq, hp), lambda b, p, i, pt, bs: (b, C_SV // hp + p)),
                       pl.BlockSpec((km.shape[0], hp), lambda b, p, i, pt, bs: (0, p)),
                       pl.BlockSpec((km.shape[0], hp), lambda b, p, i, pt, bs: (0, p)),
                       pl.BlockSpec((SB_BK, SB_BK), lambda b, p, i, pt, bs: (0, 0)),
                       new_tok(C_SQ), new_tok(C_SK), new_tok(C_SV)]
                      + [page_spec(g) for g in range(n_grp)] + [page_spec(g) for g in range(n_grp)]
                      + [pl.BlockSpec((rows, 1), lambda b, p, i, pt, bs: (0, 0))]),
            out_specs=(pl.BlockSpec((SB_BQ, hp), lambda b, p, i, pt, bs: (b * nq + i, p)),
                       new_tok(0)),
            scratch_shapes=[pltpu.VMEM((2, SB_BQ, hp), f32), pltpu.VMEM((2, SB_BQ, 1), f32),
                            pltpu.VMEM((rows, SB_W), bf16), pltpu.VMEM((rows, SB_W), f32),
                            pltpu.VMEM((rows, 1), f32), pltpu.VMEM((2, page, SB_W), f32)]),
        compiler_params=_cparams(("arbitrary", "arbitrary", "arbitrary")),
        name="stick_breaking",
    )(page_table, bias, proj, proj, proj, km, vm, tri, proj3, proj3, proj3,
      *([ck] * n_grp), *([cv] * n_grp), bias_col)


def _sb_meta_kernel(bias_ref, q_ref, km_ref, vm_ref, tri_ref, o_ref):
    p = pl.program_id(0)
    nq, nm = q_ref.shape[0], km_ref.shape[0]
    q = q_ref[...] * (SB_DH ** -0.5 * LOG2E)
    first = lax.broadcasted_iota(jnp.int32, (nq, 2 * SB_DH), 1) < SB_DH
    mask = lax.broadcasted_iota(jnp.int32, (nq, nm), 1) < lax.broadcasted_iota(jnp.int32, (nq, nm), 0)
    tri = tri_ref[...][:nm, :nm]
    kb, vb = km_ref[...], vm_ref[...]
    out = []
    for hh, qh in enumerate((jnp.where(first, q, 0.0), jnp.where(first, 0.0, q))):
        z2 = _dot_nt(qh.astype(bf16), kb) + bias_ref[2 * p + hh] * LOG2E
        w, _ = _sb_terms(z2, tri, mask)
        out.append(_dot(w, vb))
    o_ref[...] = jnp.where(first, out[0], out[1]).astype(bf16)


def _sb_meta(proj_small, row0, km, vm, bias, tri):
    hp = 2 * SB_DH
    nm = km.shape[0]
    return pl.pallas_call(
        _sb_meta_kernel,
        out_shape=jax.ShapeDtypeStruct((N_META, SB_W), bf16),
        grid_spec=pltpu.PrefetchScalarGridSpec(
            num_scalar_prefetch=1,
            grid=(SB_HEADS // 2,),
            in_specs=[pl.BlockSpec((N_META, hp), lambda p, s: (row0 // N_META, C_SQ // hp + p)),
                      pl.BlockSpec((nm, hp), lambda p, s: (0, p)),
                      pl.BlockSpec((nm, hp), lambda p, s: (0, p)),
                      pl.BlockSpec((SB_BK, SB_BK), lambda p, s: (0, 0))],
            out_specs=pl.BlockSpec((N_META, hp), lambda p, s: (0, p))),
        compiler_params=_cparams(("parallel",)),
        name="stick_breaking_meta",
    )(bias, proj_small, km, vm, tri)


def _gate_mix_kernel(ret_ref, sb_ref, gr_ref, gs_ref, wr_ref, ws_ref, o_ref):
    m = _sigmoid(gr_ref[...]) * _dot(ret_ref[...], wr_ref[...]) + _sigmoid(gs_ref[...]) * _dot(sb_ref[...], ws_ref[...])
    o_ref[...] = m.astype(bf16)


def _gate_mix(ret, sb, proj, wr, ws, tm, tn):
    m = ret.shape[0]
    return pl.pallas_call(
        _gate_mix_kernel,
        out_shape=jax.ShapeDtypeStruct((m, D_MODEL), bf16),
        grid=(m // tm, D_MODEL // tn),
        in_specs=[pl.BlockSpec((tm, RET_V), lambda i, j: (i, 0)),
                  pl.BlockSpec((tm, SB_W), lambda i, j: (i, 0)),
                  pl.BlockSpec((tm, tn), lambda i, j: (i, C_GR // tn + j)),
                  pl.BlockSpec((tm, tn), lambda i, j: (i, C_GS // tn + j)),
                  pl.BlockSpec((RET_V, tn), lambda i, j: (0, j)),
                  pl.BlockSpec((SB_W, tn), lambda i, j: (0, j))],
        out_specs=pl.BlockSpec((tm, tn), lambda i, j: (i, j)),
        compiler_params=_cparams(("parallel", "arbitrary")),
        name="gate_mix",
    )(ret, sb, proj, proj, wr, ws)


def _out_proj_kernel(x_ref, m_ref, wo_ref, g_ref, x1_ref, h_ref):
    x1 = x_ref[...] + _dot(m_ref[...], wo_ref[...])
    x1_ref[...] = x1
    h_ref[...] = _rmsnorm(x1, g_ref[...]).astype(bf16)


def _out_proj(x, m, wo, g, tm):
    rows = x.shape[0]
    return pl.pallas_call(
        _out_proj_kernel,
        out_shape=(jax.ShapeDtypeStruct((rows, D_MODEL), f32), jax.ShapeDtypeStruct((rows, D_MODEL), bf16)),
        grid=(rows // tm,),
        in_specs=[pl.BlockSpec((tm, D_MODEL), lambda i: (i, 0)),
                  pl.BlockSpec((tm, D_MODEL), lambda i: (i, 0)),
                  pl.BlockSpec((D_MODEL, D_MODEL), lambda i: (0, 0)),
                  pl.BlockSpec((1, D_MODEL), lambda i: (0, 0))],
        out_specs=(pl.BlockSpec((tm, D_MODEL), lambda i: (i, 0)), pl.BlockSpec((tm, D_MODEL), lambda i: (i, 0))),
        compiler_params=_cparams(("parallel",)),
        name="out_proj_norm",
    )(x, m, wo, g)


def _gelu_tanh(x):
    return 0.5 * x * (1.0 + jnp.tanh(0.7978845608028654 * (x + 0.044715 * (x * x * x))))


def _ffn_finish(c, act, wd_ref, x_ref, gf_ref, y_ref, acc_ref):
    contrib = _dot(act.astype(bf16), wd_ref[...])

    @pl.when(c == 0)
    def _():
        acc_ref[...] = contrib

    @pl.when(c > 0)
    def _():
        acc_ref[...] += contrib

    @pl.when(c == pl.num_programs(1) - 1)
    def _():
        y_ref[...] = _rmsnorm(x_ref[...] + acc_ref[...], gf_ref[...])


def _ffn_prompt_kernel(h_ref, x_ref, wa_ref, wg_ref, cwa_ref, cwg_ref, pa_ref, pg_ref, wd_ref, gf_ref, y_ref, ta_ref, tg_ref,
                       acc_ref, prev_ref, act_ref, *, tiles_per_seq, n_chunks):
    i, c = pl.program_id(0), pl.program_id(1)
    tm = h_ref.shape[0]
    seq_start = (i % tiles_per_seq) == 0
    row = lax.broadcasted_iota(jnp.int32, (tm, FF_CHUNK), 0)

    def conv_half(h, w_ref, cw_ref, first_ref, slot, tail_ref):
        u = _dot(h, w_ref[...])
        prev = jnp.where(seq_start, first_ref[...], prev_ref[slot])
        u1 = jnp.where(row == 0, prev[7:8, :], pltpu.roll(u, 1, 0))
        u2 = jnp.where(row == 0, prev[6:7, :], jnp.where(row == 1, prev[7:8, :], pltpu.roll(u, 2, 0)))
        cw = cw_ref[...]
        tail = u[tm - 8:, :]
        prev_ref[slot] = tail
        tail_ref[...] = tail
        return cw[2:3, :] * u + cw[1:2, :] * u1 + cw[0:1, :] * u2

    def up():
        h = h_ref[...]
        a = conv_half(h, wa_ref, cwa_ref, pa_ref, c, ta_ref)
        g = conv_half(h, wg_ref, cwg_ref, pg_ref, n_chunks + c, tg_ref)
        act_ref[c % 2] = (_gelu_tanh(a) * g).astype(bf16)

    def down():
        return _dot(act_ref[(c + 1) % 2], wd_ref[...])

    @pl.when(c == 0)
    def _():
        acc_ref[...] = jnp.zeros_like(acc_ref)
        up()

    @pl.when((c > 0) & (c < n_chunks))
    def _():
        acc_ref[...] += down()
        up()

    @pl.when(c == n_chunks)
    def _():
        y_ref[...] = _rmsnorm(x_ref[...] + acc_ref[...] + down(), gf_ref[...])


def _ffn_prompt(h, x1, wup, cw, first_a, first_g, wd, gf, tm, rows_per_seq):
    rows = h.shape[0]
    fc = FF_CHUNK
    nch = D_FF // fc
    nt = rows // tm
    up_c = lambda c: jnp.minimum(c, nch - 1)
    return pl.pallas_call(
        functools.partial(_ffn_prompt_kernel, tiles_per_seq=rows_per_seq // tm, n_chunks=nch),
        out_shape=(jax.ShapeDtypeStruct((rows, D_MODEL), f32),
                   jax.ShapeDtypeStruct((nt, 8, D_FF), f32), jax.ShapeDtypeStruct((nt, 8, D_FF), f32)),
        grid=(nt, nch + 1),
        in_specs=[pl.BlockSpec((tm, D_MODEL), lambda i, c: (i, 0)),
                  pl.BlockSpec((tm, D_MODEL), lambda i, c: (i, 0)),
                  pl.BlockSpec((D_MODEL, fc), lambda i, c: (0, up_c(c))),
                  pl.BlockSpec((D_MODEL, fc), lambda i, c: (0, nch + up_c(c))),
                  pl.BlockSpec((CONV_W, fc), lambda i, c: (0, up_c(c))),
                  pl.BlockSpec((CONV_W, fc), lambda i, c: (0, nch + up_c(c))),
                  pl.BlockSpec((8, fc), lambda i, c: (0, up_c(c))),
                  pl.BlockSpec((8, fc), lambda i, c: (0, up_c(c))),
                  pl.BlockSpec((fc, D_MODEL), lambda i, c: (jnp.maximum(c - 1, 0), 0)),
                  pl.BlockSpec((1, D_MODEL), lambda i, c: (0, 0))],
        out_specs=(pl.BlockSpec((tm, D_MODEL), lambda i, c: (i, 0)),
                   pl.BlockSpec((None, 8, fc), lambda i, c: (i, 0, up_c(c))),
                   pl.BlockSpec((None, 8, fc), lambda i, c: (i, 0, up_c(c)))),
        scratch_shapes=[pltpu.VMEM((tm, D_MODEL), f32), pltpu.VMEM((2 * nch, 8, fc), f32),
                        pltpu.VMEM((2, tm, fc), bf16)],
        compiler_params=_cparams(("arbitrary", "arbitrary")),
        name="conv_ffn_prompt",
    )(h, x1, wup, wup, cw, cw, first_a, first_g, wd, gf)


def _ffn_small_kernel(h_ref, x_ref, wa_ref, wg_ref, cwa_ref, cwg_ref, sta_ref, stg_ref, wd_ref, gf_ref,
                      y_ref, ua_ref, ug_ref, acc_ref, *, n_sample, n_tok):
    c = pl.program_id(1)
    rows = h_ref.shape[0]
    h = h_ref[...]
    row = lax.broadcasted_iota(jnp.int32, (rows, FF_CHUNK), 0)
    t = jnp.where(row < n_sample, row % n_tok, row - n_sample)

    def conv_half(w_ref, cw_ref, st_ref, u_ref):
        u = _dot(h, w_ref[...])
        u_ref[...] = u
        st = st_ref[...]
        u1 = jnp.where(t == 0, pltpu.roll(st, rows - 1, 0), pltpu.roll(u, 1, 0))
        u2 = jnp.where(t < 2, st, pltpu.roll(u, 2, 0))
        cw = cw_ref[...]
        return cw[2:3, :] * u + cw[1:2, :] * u1 + cw[0:1, :] * u2

    a = conv_half(wa_ref, cwa_ref, sta_ref, ua_ref)
    g = conv_half(wg_ref, cwg_ref, stg_ref, ug_ref)
    _ffn_finish(c, _gelu_tanh(a) * g, wd_ref, x_ref, gf_ref, y_ref, acc_ref)


def _ffn_small(h, x1, wup, cw, st, wd, gf, n_sample, n_tok):
    rows = h.shape[0]
    fc = FF_CHUNK
    nch = D_FF // fc
    return pl.pallas_call(
        functools.partial(_ffn_small_kernel, n_sample=n_sample, n_tok=n_tok),
        out_shape=(jax.ShapeDtypeStruct((rows, D_MODEL), f32),
                   jax.ShapeDtypeStruct((rows, D_FF), f32), jax.ShapeDtypeStruct((rows, D_FF), f32)),
        grid=(1, nch),
        in_specs=[pl.BlockSpec((rows, D_MODEL), lambda i, c: (0, 0)),
                  pl.BlockSpec((rows, D_MODEL), lambda i, c: (0, 0)),
                  pl.BlockSpec((D_MODEL, fc), lambda i, c: (0, c)),
                  pl.BlockSpec((D_MODEL, fc), lambda i, c: (0, nch + c)),
                  pl.BlockSpec((CONV_W, fc), lambda i, c: (0, c)),
                  pl.BlockSpec((CONV_W, fc), lambda i, c: (0, nch + c)),
                  pl.BlockSpec((rows, fc), lambda i, c: (0, c)),
                  pl.BlockSpec((rows, fc), lambda i, c: (0, nch + c)),
                  pl.BlockSpec((fc, D_MODEL), lambda i, c: (c, 0)),
                  pl.BlockSpec((1, D_MODEL), lambda i, c: (0, 0))],
        out_specs=(pl.BlockSpec((rows, D_MODEL), lambda i, c: (0, 0)),
                   pl.BlockSpec((rows, fc), lambda i, c: (0, c)),
                   pl.BlockSpec((rows, fc), lambda i, c: (0, c))),
        scratch_shapes=[pltpu.VMEM((rows, D_MODEL), f32)],
        compiler_params=_cparams(("arbitrary", "arbitrary")),
        name="conv_ffn_small",
    )(h, x1, wup, wup, cw, cw, st, st, wd, gf)


def _rope_tables(pos):
    half = RET_DK // 2
    inv = ROPE_BASE ** (-jnp.arange(half, dtype=f32) / half)
    ang = pos.astype(f32)[:, None] * inv[None, :]
    c, s = jnp.cos(ang), jnp.sin(ang)
    return jnp.concatenate([c, c], axis=-1), jnp.concatenate([-s, s], axis=-1)


def _layer(xp, xs, ck, cv, page_table, s_ret, s_conv, meta, norm1_g, w_in, sb_bias, w_ret_proj, w_sb_proj, w_o,
           norm2_g, w_up, conv_w, w_down, norm_f_g):
    batch, seq, d = xp.shape
    nb, n_tok, _ = xs.shape
    n_pool, page = ck.shape[0], ck.shape[1]
    n_pages = page_table.shape[1]
    past = n_pages * page
    n_sample = nb * n_tok
    n_small = n_sample + N_META

    w_in, w_ret_proj, w_sb_proj, w_o, w_up, w_down = (
        w.astype(bf16) for w in (w_in, w_ret_proj, w_sb_proj, w_o, w_up, w_down))
    g1, g2, gf = norm1_g[None, :], norm2_g[None, :], norm_f_g[None, :]
    x_big = xp.reshape(batch * seq, d)
    x_small = jnp.concatenate([xs.reshape(n_sample, d), meta.astype(f32)], axis=0)

    tm_big = 1024
    proj_s = _norm_mm(x_small, g1, w_in, n_small, 1024)
    proj_p = _norm_mm(x_big, g1, w_in, tm_big, 1024)

    lg = jnp.log1p(-jnp.exp2(-5.0 - jnp.arange(RET_HEADS, dtype=f32)))
    lg = jnp.broadcast_to(lg[:, None, None], (RET_HEADS, 1, RET_DV))
    cos_m, sin_m = _rope_tables(jnp.arange(N_META))
    ret_m, s_meta = _ret_small(proj_s, n_sample, N_META, N_META, cos_m, sin_m, lg,
                               jnp.zeros((1, RET_HEADS, RET_DK, RET_DV), f32))
    cos_p, sin_p = _rope_tables(N_META + jnp.arange(seq))
    ret_p, sret_p = _ret_prompt(proj_p, cos_p, sin_p, lg, s_meta[0], batch, seq)
    cos_s, sin_s = _rope_tables(past + (jnp.arange(SMALL_ROWS) % n_tok))
    ret_s, sret_s = _ret_small(proj_s, 0, n_sample, n_tok, cos_s, sin_s, lg, s_ret)

    tri = (jnp.arange(SB_BK)[:, None] > jnp.arange(SB_BK)[None, :]).astype(bf16)
    pad_rows = lambda a, n: jnp.pad(a, [(0, 0)] * (a.ndim - 2) + [(0, n - a.shape[-2]), (0, 0)])
    km = pad_rows(proj_s[n_sample:, C_SK:C_SK + SB_W], 128).astype(bf16)
    vm = pad_rows(proj_s[n_sample:, C_SV:C_SV + SB_W], 128).astype(bf16)
    sb_m = _sb_meta(proj_s, n_sample, km, vm, sb_bias, tri)
    ps3 = proj_s[:n_sample].reshape(nb, n_tok, IN_WIDTH)
    cache_t = lambda c: jnp.transpose(c, (0, 2, 3, 1)).reshape(n_pool, SB_W, page)
    bias_col = jnp.tile(sb_bias * LOG2E, n_tok)[:, None]
    sb_p, sb_s = _stick_breaking(proj_p, km, vm, sb_bias, tri, batch, seq,
                                 page_table, ps3, cache_t(ck), cache_t(cv), bias_col)

    ret_small = jnp.concatenate([ret_s, ret_m], axis=0)
    sb_small = jnp.concatenate([sb_s.reshape(n_sample, SB_W).astype(bf16), sb_m], axis=0)
    m_s = _gate_mix(ret_small, sb_small, proj_s, w_ret_proj, w_sb_proj, n_small, 512)
    m_p = _gate_mix(ret_p, sb_p, proj_p, w_ret_proj, w_sb_proj, tm_big, 512)
    x1_s, h2_s = _out_proj(x_small, m_s, w_o, g2, n_small)
    x1_p, h2_p = _out_proj(x_big, m_p, w_o, g2, 512)

    st = jnp.concatenate([jnp.pad(s_conv, ((0, 0), (0, n_tok - (CONV_W - 1)), (0, 0))).reshape(n_sample, 2 * D_FF),
                          jnp.zeros((N_META, 2 * D_FF), f32)], axis=0)
    y_s, ua_s, ug_s = _ffn_small(h2_s, x1_s, w_up, conv_w, st, w_down, gf, n_sample, n_tok)
    y_p, ta_p, tg_p = _ffn_prompt(h2_p, x1_p, w_up, conv_w, ua_s[n_small - 8:], ug_s[n_small - 8:], w_down, gf, 512, seq)

    y_prompt = y_p.reshape(batch, seq, d)
    y_sample = y_s[:n_sample].reshape(nb, n_tok, d)

    def with_meta(col):
        real = proj_p[:, col:col + SB_W].reshape(batch, seq, SB_HEADS, SB_DH)
        m = jnp.broadcast_to(proj_s[n_sample:, col:col + SB_W].reshape(1, N_META, SB_HEADS, SB_DH),
                             (batch, N_META, SB_HEADS, SB_DH))
        return jnp.concatenate([m, real], axis=1)

    k_p, v_p = with_meta(C_SK), with_meta(C_SV)
    k_s = ps3[:, :, C_SK:C_SK + SB_W].reshape(nb, n_tok, SB_HEADS, SB_DH)
    v_s = ps3[:, :, C_SV:C_SV + SB_W].reshape(nb, n_tok, SB_HEADS, SB_DH)
    tiles_per_seq = seq // 512
    last = lambda tl: tl.reshape(batch, tiles_per_seq, 8, D_FF)[:, -1, 8 - (CONV_W - 1):, :]
    conv_p = jnp.concatenate([last(ta_p), last(tg_p)], axis=-1)
    u_s = jnp.concatenate([ua_s[:n_sample], ug_s[:n_sample]], axis=-1).reshape(nb, n_tok, 2 * D_FF)
    conv_s = u_s[:, n_tok - (CONV_W - 1):, :]
    return y_prompt, y_sample, k_p, v_p, k_s, v_s, sret_p, sret_s, conv_p, conv_s


def kernel(x_prompt, x_sample, cache_k, cache_v, page_table, state_ret, state_conv, meta_tokens, norm1_g, w_in,
           sb_bias, w_ret_proj, w_sb_proj, w_o, norm2_g, w_up, conv_w, w_down, norm_f_g):
    assert cache_k.shape[0] == 1, "one layer"
    outs = _layer(x_prompt, x_sample, cache_k[0], cache_v[0], page_table, state_ret[0], state_conv[0], meta_tokens,
                  norm1_g[0], w_in[0], sb_bias[0], w_ret_proj[0], w_sb_proj[0], w_o[0], norm2_g[0], w_up[0],
                  conv_w[0], w_down[0], norm_f_g)
    y_prompt, y_sample = outs[0], outs[1]
    return (y_prompt, y_sample) + tuple(o[None] for o in outs[2:])
```

```python
import functools

import jax
import jax.numpy as jnp
from jax import lax
from jax.experimental import pallas as pl
from jax.experimental.pallas import tpu as pltpu

f32 = jnp.float32
bf16 = jnp.bfloat16

D_MODEL = 2048
N_META = 16
RET_HEADS = 8
RET_DK = 128
RET_DV = 256
SB_HEADS = 16
SB_DH = 64
D_FF = 5632
CONV_W = 3
ROPE_BASE = 10000.0
EPS = 1e-6
GN_EPS = 1e-5

RET_QK = RET_HEADS * RET_DK
RET_V = RET_HEADS * RET_DV
SB_W = SB_HEADS * SB_DH
C_RQ, C_RK, C_RV, C_RG = 0, RET_QK, 2 * RET_QK, 2 * RET_QK + RET_V
C_SQ = C_RG + RET_V
C_SK, C_SV = C_SQ + SB_W, C_SQ + 2 * SB_W
C_GR = C_SV + SB_W
C_GS = C_GR + D_MODEL
IN_WIDTH = C_GS + D_MODEL

RET_CHUNK = 128
RET_HEAD_GROUP = 8
SB_BQ = 512
SB_BK = 256
LOG2E = 1.4426950408889634
SB_LOGIT_CLAMP = 100.0
SMALL_ROWS = 16
FF_CHUNK = 512
VMEM_LIMIT = 56 * 1024 * 1024


def _cparams(sem):
    return pltpu.CompilerParams(dimension_semantics=sem, vmem_limit_bytes=VMEM_LIMIT)


def _dot(a, b):
    return jnp.dot(a, b, preferred_element_type=f32)


def _dot_nt(a, b):
    return lax.dot_general(a, b, (((1,), (1,)), ((), ())), preferred_element_type=f32)


def _sigmoid(x):
    return 1.0 / (1.0 + jnp.exp(-x))


def _rmsnorm(x, g):
    return x * lax.rsqrt(jnp.mean(x * x, axis=-1, keepdims=True) + EPS) * g


def _norm_mm_kernel(x_ref, g_ref, w_ref, o_ref, h_ref):
    @pl.when(pl.program_id(1) == 0)
    def _():
        h_ref[...] = _rmsnorm(x_ref[...], g_ref[...]).astype(bf16)

    o_ref[...] = _dot(h_ref[...], w_ref[...])


def _norm_mm(x, g, w, tm, tn):
    m, d = x.shape
    n = w.shape[1]
    return pl.pallas_call(
        _norm_mm_kernel,
        out_shape=jax.ShapeDtypeStruct((m, n), f32),
        grid=(m // tm, n // tn),
        in_specs=[pl.BlockSpec((tm, d), lambda i, j: (i, 0)),
                  pl.BlockSpec((1, d), lambda i, j: (0, 0)),
                  pl.BlockSpec((d, tn), lambda i, j: (0, j))],
        out_specs=pl.BlockSpec((tm, tn), lambda i, j: (i, j)),
        scratch_shapes=[pltpu.VMEM((tm, d), bf16)],
        compiler_params=_cparams(("parallel", "arbitrary")),
        name="norm_in_proj",
    )(x, g, w)


def _rope(x, cos2, sin2):
    return x * cos2 + pltpu.roll(x, RET_DK // 2, 1) * sin2


def _gated_groupnorm(o, gate):
    mu = jnp.mean(o, axis=-1, keepdims=True)
    oc = o - mu
    var = jnp.mean(oc * oc, axis=-1, keepdims=True)
    return (gate * _sigmoid(gate)) * (oc * lax.rsqrt(var + GN_EPS))


def _ret_prompt_kernel(q_ref, k_ref, v_ref, g_ref, cos_ref, sin_ref, lg_ref, s0_ref, o_ref, sfin_ref, s_ref):
    c = pl.program_id(2)
    n = RET_CHUNK

    @pl.when(c == 0)
    def _():
        s_ref[...] = s0_ref[...]

    row = lax.broadcasted_iota(jnp.int32, (n, RET_DK), 0).astype(f32)
    col = lax.broadcasted_iota(jnp.int32, (n, n), 1).astype(f32)
    diff = row - col
    cos2, sin2 = cos_ref[...], sin_ref[...]
    for h in range(RET_HEAD_GROUP):
        dk = slice(h * RET_DK, (h + 1) * RET_DK)
        dv = slice(h * RET_DV, (h + 1) * RET_DV)
        lgv = lg_ref[h]
        lg = lgv[:, :RET_DK]
        q = _rope(q_ref[:, dk], cos2, sin2)
        k = _rope(k_ref[:, dk], cos2, sin2) * (RET_DK ** -0.5)
        decay = jnp.where(diff >= 0, jnp.exp(diff * lg), 0.0)
        att = _dot_nt(q.astype(bf16), k.astype(bf16)) * decay
        vb = v_ref[:, dv].astype(bf16)
        s = s_ref[h]
        o = _dot(att.astype(bf16), vb) + _dot((q * jnp.exp((row + 1.0) * lg)).astype(bf16), s.astype(bf16))
        kd = k * jnp.exp((n - 1.0 - row) * lg)
        s_ref[h] = jnp.exp(n * lgv) * s + _dot(kd.T.astype(bf16), vb)
        o_ref[:, dv] = _gated_groupnorm(o, g_ref[:, dv]).astype(bf16)

    @pl.when(c == pl.num_programs(2) - 1)
    def _():
        sfin_ref[...] = s_ref[...]


def _ret_prompt(proj, cos2, sin2, lg, s0, batch, seq):
    nc = seq // RET_CHUNK
    n = RET_CHUNK
    hg = RET_HEAD_GROUP
    wk, wv = hg * RET_DK, hg * RET_DV
    row = lambda b, h, c: b * nc + c
    return pl.pallas_call(
        _ret_prompt_kernel,
        out_shape=(jax.ShapeDtypeStruct((batch * seq, RET_V), bf16),
                   jax.ShapeDtypeStruct((batch, RET_HEADS, RET_DK, RET_DV), f32)),
        grid=(batch, RET_HEADS // hg, nc),
        in_specs=[pl.BlockSpec((n, wk), lambda b, h, c: (row(b, h, c), C_RQ // wk + h)),
                  pl.BlockSpec((n, wk), lambda b, h, c: (row(b, h, c), C_RK // wk + h)),
                  pl.BlockSpec((n, wv), lambda b, h, c: (row(b, h, c), C_RV // wv + h)),
                  pl.BlockSpec((n, wv), lambda b, h, c: (row(b, h, c), C_RG // wv + h)),
                  pl.BlockSpec((n, RET_DK), lambda b, h, c: (c, 0)),
                  pl.BlockSpec((n, RET_DK), lambda b, h, c: (c, 0)),
                  pl.BlockSpec((hg, 1, RET_DV), lambda b, h, c: (h, 0, 0)),
                  pl.BlockSpec((hg, RET_DK, RET_DV), lambda b, h, c: (h, 0, 0))],
        out_specs=(pl.BlockSpec((n, wv), lambda b, h, c: (row(b, h, c), h)),
                   pl.BlockSpec((None, hg, RET_DK, RET_DV), lambda b, h, c: (b, h, 0, 0))),
        scratch_shapes=[pltpu.VMEM((hg, RET_DK, RET_DV), f32)],
        compiler_params=_cparams(("parallel", "parallel", "arbitrary")),
        name="retention_prompt",
    )(proj, proj, proj, proj, cos2, sin2, lg, s0)


def _ret_small_kernel(q_ref, k_ref, v_ref, g_ref, cos_ref, sin_ref, lg_ref, s0_ref, o_ref, s_ref, *, n_tok):
    r = SMALL_ROWS
    n_seq = r // n_tok
    ri = lax.broadcasted_iota(jnp.int32, (r, RET_DK), 0)
    ti = (ri % n_tok).astype(f32)
    rj = lax.broadcasted_iota(jnp.int32, (r, 128), 1)
    rr = lax.broadcasted_iota(jnp.int32, (r, 128), 0)
    tdiff = ((rr % n_tok) - (rj % n_tok)).astype(f32)
    visible = ((rr // n_tok) == (rj // n_tok)) & (tdiff >= 0)
    rk = lax.broadcasted_iota(jnp.int32, (128, RET_DK), 0)
    tk = (rk % n_tok).astype(f32)
    rv = lax.broadcasted_iota(jnp.int32, (r, RET_DV), 0)
    cos2, sin2 = cos_ref[...], sin_ref[...]
    pad = 128 - r
    for h in range(RET_HEADS):
        lgv = lg_ref[h]
        lg = lgv[:, :RET_DK]
        q = _rope(q_ref[:, h * RET_DK:(h + 1) * RET_DK], cos2, sin2)
        k = _rope(k_ref[:, h * RET_DK:(h + 1) * RET_DK], cos2, sin2) * (RET_DK ** -0.5)
        kpad = jnp.concatenate([k, jnp.zeros((pad, RET_DK), f32)], axis=0)
        vpad = jnp.concatenate([v_ref[:, h * RET_DV:(h + 1) * RET_DV], jnp.zeros((pad, RET_DV), f32)],
                               axis=0).astype(bf16)
        decay = jnp.where(visible, jnp.exp(tdiff * lg), 0.0)
        att = _dot_nt(q.astype(bf16), kpad.astype(bf16)) * decay
        o = _dot(att.astype(bf16), vpad)
        qd = (q * jnp.exp((ti + 1.0) * lg)).astype(bf16)
        kd = kpad * jnp.exp((n_tok - 1.0 - tk) * lg)
        for s_i in range(n_seq):
            s = s0_ref[s_i, h]
            mine_k = (rk // n_tok) == s_i
            mine_v = (rv // n_tok) == s_i
            o = o + jnp.where(mine_v, _dot(qd, s.astype(bf16)), 0.0)
            s_ref[s_i, h] = jnp.exp(n_tok * lgv) * s + _dot(jnp.where(mine_k, kd, 0.0).T.astype(bf16), vpad)
        o_ref[:, h * RET_DV:(h + 1) * RET_DV] = _gated_groupnorm(
            o, g_ref[:, h * RET_DV:(h + 1) * RET_DV]).astype(bf16)


def _ret_small(proj, row0, n_rows, n_tok, cos2, sin2, lg, s0):
    r = SMALL_ROWS
    n_seq = r // n_tok
    steps = n_rows // r
    b0 = row0 // r
    return pl.pallas_call(
        functools.partial(_ret_small_kernel, n_tok=n_tok),
        out_shape=(jax.ShapeDtypeStruct((n_rows, RET_V), bf16),
                   jax.ShapeDtypeStruct(s0.shape, f32)),
        grid=(steps,),
        in_specs=[pl.BlockSpec((r, RET_QK), lambda i: (b0 + i, C_RQ // RET_QK)),
                  pl.BlockSpec((r, RET_QK), lambda i: (b0 + i, C_RK // RET_QK)),
                  pl.BlockSpec((r, RET_V), lambda i: (b0 + i, C_RV // RET_V)),
                  pl.BlockSpec((r, RET_V), lambda i: (b0 + i, C_RG // RET_V)),
                  pl.BlockSpec((r, RET_DK), lambda i: (0, 0)),
                  pl.BlockSpec((r, RET_DK), lambda i: (0, 0)),
                  pl.BlockSpec((RET_HEADS, 1, RET_DV), lambda i: (0, 0, 0)),
                  pl.BlockSpec((n_seq, RET_HEADS, RET_DK, RET_DV), lambda i: (i, 0, 0, 0))],
        out_specs=(pl.BlockSpec((r, RET_V), lambda i: (i, 0)),
                   pl.BlockSpec((n_seq, RET_HEADS, RET_DK, RET_DV), lambda i: (i, 0, 0, 0))),
        compiler_params=_cparams(("parallel",)),
        name="retention_small",
    )(proj, proj, proj, proj, cos2, sin2, lg, s0)


def _sb_log2_keep(z2, mask):
    zc = jnp.minimum(z2, SB_LOGIT_CLAMP)
    lk = jnp.log(1.0 + jnp.exp2(zc)) * -LOG2E
    return zc, (lk if mask is None else jnp.where(mask, lk, 0.0))


def _sb_terms(z2, tri, mask):
    zc, lk = _sb_log2_keep(z2, mask)
    between = _dot(lk.astype(bf16), tri)
    w = jnp.exp2(lk + between + zc)
    if mask is not None:
        w = jnp.where(mask, w, 0.0)
    return w.astype(bf16), between[:, :1] + lk[:, :1]


def _sb_kernel(pt_ref, bias_ref, q_ref, k_ref, v_ref, km_ref, vm_ref, tri_ref, qs_ref, kn_ref, vn_ref, *rest,
               n_tok, n_grp):
    k_refs, v_refs = rest[:n_grp], rest[n_grp:2 * n_grp]
    bcol_ref, o_ref, os_ref, acc_ref, r_ref, qbd_ref, accs_ref, rs_ref, new_ref = rest[2 * n_grp:]
    p, i = pl.program_id(1), pl.program_id(2)
    bq, bk = SB_BQ, SB_BK
    tri = tri_ref[...]

    page = k_refs[0].shape[1]
    rows = n_tok * SB_HEADS
    head_of_col = lax.broadcasted_iota(jnp.int32, (SB_HEADS, SB_W), 1) // SB_DH
    own = head_of_col == lax.broadcasted_iota(jnp.int32, (SB_HEADS, SB_W), 0)
    tri_p = tri[:page, :page]

    def sample_blocks(logits, weigh, mask, init):
        zc, lk = zip(*[_sb_log2_keep(z2 + bcol_ref[...], mask) for z2 in logits])
        between = _dot(jnp.concatenate([l.astype(bf16) for l in lk], axis=0), tri_p)
        r = jnp.zeros((rows, 1), f32) if init else rs_ref[...]
        acc = None if init else accs_ref[...]
        for g in range(len(logits)):
            btw = between[g * rows:(g + 1) * rows]
            w = jnp.exp2(lk[g] + btw + r + zc[g])
            if mask is not None:
                w = jnp.where(mask, w, 0.0)
            r = r + btw[:, :1] + lk[g][:, :1]
            contrib = weigh[g](w.astype(bf16))
            acc = contrib if acc is None else acc + contrib
        rs_ref[...] = r
        accs_ref[...] = acc

    @pl.when(i == 0)
    def _():
        qs = qs_ref[...] * (SB_DH ** -0.5 * LOG2E)
        for t in range(n_tok):
            qbd_ref[t * SB_HEADS:(t + 1) * SB_HEADS, :] = jnp.where(own, qs[t:t + 1, :], 0.0).astype(bf16)
        new_ref[...] = jnp.zeros_like(new_ref)
        new_ref[0, :n_tok, :] = kn_ref[...]
        new_ref[1, :n_tok, :] = vn_ref[...]
        kb, vb = new_ref[0].astype(bf16), new_ref[1].astype(bf16)
        kk = lax.broadcasted_iota(jnp.int32, (rows, page), 1)
        tq = lax.broadcasted_iota(jnp.int32, (rows, page), 0) // SB_HEADS
        sample_blocks([_dot_nt(qbd_ref[...], kb)], [lambda w: _dot(w, vb)], kk < tq, True)

    qbd = qbd_ref[...]
    sample_blocks([_dot(qbd, r_[...].astype(bf16)) for r_ in k_refs],
                  [lambda w, r_=r_: _dot_nt(w, r_[...].astype(bf16)) for r_ in v_refs], None, False)

    q = q_ref[...] * (SB_DH ** -0.5 * LOG2E)
    lane = lax.broadcasted_iota(jnp.int32, (bq, 2 * SB_DH), 1)
    first = lane < SB_DH
    qm = (jnp.where(first, q, 0.0).astype(bf16), jnp.where(first, 0.0, q).astype(bf16))
    bias = (bias_ref[2 * p] * LOG2E, bias_ref[2 * p + 1] * LOG2E)

    def terms(kb, vb, mask, tri_b):
        out = []
        for hh in range(2):
            w, total = _sb_terms(_dot_nt(qm[hh], kb) + bias[hh], tri_b, mask)
            out.append((_dot(w, vb), total))
        return out

    def accumulate(res):
        for hh, (contrib, total) in enumerate(res):
            r = r_ref[hh]
            acc_ref[hh] += jnp.exp2(r) * contrib
            r_ref[hh] = r + total

    def kv(j):
        s0 = pl.multiple_of(j * bk, bk)
        return k_ref[pl.ds(s0, bk), :].astype(bf16), v_ref[pl.ds(s0, bk), :].astype(bf16)

    nkb = bq // bk
    kk = lax.broadcasted_iota(jnp.int32, (bq, bk), 1)
    qq = lax.broadcasted_iota(jnp.int32, (bq, bk), 0)
    for d in reversed(range(nkb)):
        res = terms(*kv(i * nkb + d), kk + d * bk < qq, tri)
        if d == nkb - 1:
            for hh, (contrib, total) in enumerate(res):
                acc_ref[hh] = contrib
                r_ref[hh] = total
        else:
            accumulate(res)

    def pair(t, carry):
        newer = terms(*kv(i * nkb - 1 - 2 * t), None, tri)
        older = terms(*kv(i * nkb - 2 - 2 * t), None, tri)
        accumulate(newer)
        accumulate(older)
        return carry

    lax.fori_loop(0, i * (nkb // 2), pair, 0)

    nm = km_ref.shape[0]
    km_valid = lax.broadcasted_iota(jnp.int32, (bq, nm), 1) < N_META
    accumulate(terms(km_ref[...], vm_ref[...], km_valid, tri[:nm, :nm]))
    o_ref[...] = jnp.where(first, acc_ref[0], acc_ref[1]).astype(bf16)

    @pl.when(i == pl.num_programs(2) - 1)
    def _():
        for t in range(n_tok):
            sel = jnp.where(own, accs_ref[t * SB_HEADS:(t + 1) * SB_HEADS, :], 0.0)
            os_ref[t:t + 1, :] = jnp.sum(sel, axis=0, keepdims=True)


def _stick_breaking(proj, km, vm, bias, tri, batch, seq, page_table, proj3, ck, cv, bias_col):
    nq = seq // SB_BQ
    hp = 2 * SB_DH
    n_pairs = SB_HEADS // 2
    nb, n_tok, _ = proj3.shape
    n_pages = page_table.shape[1]
    page = ck.shape[2]
    rows = n_tok * SB_HEADS
    assert nb == batch * n_pairs and n_pages % nq == 0
    n_grp = n_pages // nq

    def page_spec(g):
        return pl.BlockSpec((None, SB_W, page),
                            lambda b, p, i, pt, bs: (pt[b * n_pairs + p, n_pages - 1 - (i * n_grp + g)], 0, 0))

    new_tok = lambda col: pl.BlockSpec((None, n_tok, SB_W), lambda b, p, i, pt, bs: (b * n_pairs + p, 0, col // SB_W))
    return pl.pallas_call(
        functools.partial(_sb_kernel, n_tok=n_tok, n_grp=n_grp),
        out_shape=(jax.ShapeDtypeStruct((batch * seq, SB_W), bf16), jax.ShapeDtypeStruct((nb, n_tok, SB_W), f32)),
        grid_spec=pltpu.PrefetchScalarGridSpec(
            num_scalar_prefetch=2,
            grid=(batch, n_pairs, nq),
            in_specs=([pl.BlockSpec((SB_BQ, hp), lambda b, p, i, pt, bs: (b * nq + i, C_SQ // hp + p)),
                       pl.BlockSpec((seq, hp), lambda b, p, i, pt, bs: (b, C_SK // hp + p)),
                       pl.BlockSpec((seq, hp), lambda b, p, i, pt, bs: (b, C_SV // hp + p)),
                       pl.BlockSpec((km.shape[0], hp), lambda b, p, i, pt, bs: (0, p)),
                       pl.BlockSpec((km.shape[0], hp), lambda b, p, i, pt, bs: (0, p)),
                       pl.BlockSpec((SB_BK, SB_BK), lambda b, p, i, pt, bs: (0, 0)),
                       new_tok(C_SQ), new_tok(C_SK), new_tok(C_SV)]
                      + [page_spec(g) for g in range(n_grp)] + [page_spec(g) for g in range(n_grp)]
                      + [pl.BlockSpec((rows, 1), lambda b, p, i, pt, bs: (0, 0))]),
            out_specs=(pl.BlockSpec((SB_BQ, hp), lambda b, p, i, pt, bs: (b * nq + i, p)),
                       new_tok(0)),
            scratch_shapes=[pltpu.VMEM((2, SB_BQ, hp), f32), pltpu.VMEM((2, SB_BQ, 1), f32),
                            pltpu.VMEM((rows, SB_W), bf16), pltpu.VMEM((rows, SB_W), f32),
                            pltpu.VMEM((rows, 1), f32), pltpu.VMEM((2, page, SB_W), f32)]),
        compiler_params=_cparams(("arbitrary", "arbitrary", "arbitrary")),
        name="stick_breaking",
    )(page_table, bias, proj, proj, proj, km, vm, tri, proj3, proj3, proj3,
      *([ck] * n_grp), *([cv] * n_grp), bias_col)


def _sb_meta_kernel(bias_ref, q_ref, km_ref, vm_ref, tri_ref, o_ref):
    p = pl.program_id(0)
    nq, nm = q_ref.shape[0], km_ref.shape[0]
    q = q_ref[...] * (SB_DH ** -0.5 * LOG2E)
    first = lax.broadcasted_iota(jnp.int32, (nq, 2 * SB_DH), 1) < SB_DH
    mask = lax.broadcasted_iota(jnp.int32, (nq, nm), 1) < lax.broadcasted_iota(jnp.int32, (nq, nm), 0)
    tri = tri_ref[...][:nm, :nm]
    kb, vb = km_ref[...], vm_ref[...]
    out = []
    for hh, qh in enumerate((jnp.where(first, q, 0.0), jnp.where(first, 0.0, q))):
        z2 = _dot_nt(qh.astype(bf16), kb) + bias_ref[2 * p + hh] * LOG2E
        w, _ = _sb_terms(z2, tri, mask)
        out.append(_dot(w, vb))
    o_ref[...] = jnp.where(first, out[0], out[1]).astype(bf16)


def _sb_meta(proj_small, row0, km, vm, bias, tri):
    hp = 2 * SB_DH
    nm = km.shape[0]
    return pl.pallas_call(
        _sb_meta_kernel,
        out_shape=jax.ShapeDtypeStruct((N_META, SB_W), bf16),
        grid_spec=pltpu.PrefetchScalarGridSpec(
            num_scalar_prefetch=1,
            grid=(SB_HEADS // 2,),
            in_specs=[pl.BlockSpec((N_META, hp), lambda p, s: (row0 // N_META, C_SQ // hp + p)),
                      pl.BlockSpec((nm, hp), lambda p, s: (0, p)),
                      pl.BlockSpec((nm, hp), lambda p, s: (0, p)),
                      pl.BlockSpec((SB_BK, SB_BK), lambda p, s: (0, 0))],
            out_specs=pl.BlockSpec((N_META, hp), lambda p, s: (0, p))),
        compiler_params=_cparams(("parallel",)),
        name="stick_breaking_meta",
    )(bias, proj_small, km, vm, tri)


def _kv_out_kernel(mk_ref, k_ref, mv_ref, v_ref, ok_ref, ov_ref):
    seq = k_ref.shape[0]
    lanes = k_ref.shape[1]
    for m_ref, x_ref, o_ref in ((mk_ref, k_ref, ok_ref), (mv_ref, v_ref, ov_ref)):
        o_ref[:, 0:lanes] = jnp.concatenate([m_ref[...], x_ref[0:lanes - N_META, :]], axis=0).T
        for t in range(1, seq // lanes):
            o_ref[:, t * lanes:(t + 1) * lanes] = x_ref[t * lanes - N_META:(t + 1) * lanes - N_META, :].T
        last = jnp.concatenate([x_ref[seq - N_META:seq, :], jnp.zeros((lanes - N_META, lanes), f32)], axis=0)
        o_ref[:, seq:seq + N_META] = last.T[:, :N_META]


def _kv_out(proj, proj_small, meta_row0, batch, seq):
    lanes = 128
    total = N_META + seq
    meta = lambda col: pl.BlockSpec((N_META, lanes), lambda b, c: (meta_row0 // N_META, col // lanes + c))
    rows = lambda col: pl.BlockSpec((seq, lanes), lambda b, c: (b, col // lanes + c))
    out = pl.BlockSpec((None, lanes, total), lambda b, c: (b, c, 0))
    return pl.pallas_call(
        _kv_out_kernel,
        out_shape=(jax.ShapeDtypeStruct((batch, SB_W, total), f32), jax.ShapeDtypeStruct((batch, SB_W, total), f32)),
        grid=(batch, SB_W // lanes),
        in_specs=[meta(C_SK), rows(C_SK), meta(C_SV), rows(C_SV)],
        out_specs=(out, out),
        compiler_params=_cparams(("parallel", "parallel")),
        name="kv_prompt_out",
    )(proj_small, proj, proj_small, proj)


def _gate_mix_kernel(ret_ref, sb_ref, gr_ref, gs_ref, wr_ref, ws_ref, o_ref):
    m = _sigmoid(gr_ref[...]) * _dot(ret_ref[...], wr_ref[...]) + _sigmoid(gs_ref[...]) * _dot(sb_ref[...], ws_ref[...])
    o_ref[...] = m.astype(bf16)


def _gate_mix(ret, sb, proj, wr, ws, tm, tn):
    m = ret.shape[0]
    return pl.pallas_call(
        _gate_mix_kernel,
        out_shape=jax.ShapeDtypeStruct((m, D_MODEL), bf16),
        grid=(m // tm, D_MODEL // tn),
        in_specs=[pl.BlockSpec((tm, RET_V), lambda i, j: (i, 0)),
                  pl.BlockSpec((tm, SB_W), lambda i, j: (i, 0)),
                  pl.BlockSpec((tm, tn), lambda i, j: (i, C_GR // tn + j)),
                  pl.BlockSpec((tm, tn), lambda i, j: (i, C_GS // tn + j)),
                  pl.BlockSpec((RET_V, tn), lambda i, j: (0, j)),
                  pl.BlockSpec((SB_W, tn), lambda i, j: (0, j))],
        out_specs=pl.BlockSpec((tm, tn), lambda i, j: (i, j)),
        compiler_params=_cparams(("parallel", "arbitrary")),
        name="gate_mix",
    )(ret, sb, proj, proj, wr, ws)


def _out_proj_kernel(x_ref, m_ref, wo_ref, g_ref, x1_ref, h_ref):
    x1 = x_ref[...] + _dot(m_ref[...], wo_ref[...])
    x1_ref[...] = x1
    h_ref[...] = _rmsnorm(x1, g_ref[...]).astype(bf16)


def _out_proj(x, m, wo, g, tm):
    rows = x.shape[0]
    return pl.pallas_call(
        _out_proj_kernel,
        out_shape=(jax.ShapeDtypeStruct((rows, D_MODEL), f32), jax.ShapeDtypeStruct((rows, D_MODEL), bf16)),
        grid=(rows // tm,),
        in_specs=[pl.BlockSpec((tm, D_MODEL), lambda i: (i, 0)),
                  pl.BlockSpec((tm, D_MODEL), lambda i: (i, 0)),
                  pl.BlockSpec((D_MODEL, D_MODEL), lambda i: (0, 0)),
                  pl.BlockSpec((1, D_MODEL), lambda i: (0, 0))],
        out_specs=(pl.BlockSpec((tm, D_MODEL), lambda i: (i, 0)), pl.BlockSpec((tm, D_MODEL), lambda i: (i, 0))),
        compiler_params=_cparams(("parallel",)),
        name="out_proj_norm",
    )(x, m, wo, g)


def _gelu_tanh(x):
    return 0.5 * x * (1.0 + jnp.tanh(0.7978845608028654 * (x + 0.044715 * (x * x * x))))


def _ffn_finish(c, act, wd_ref, x_ref, gf_ref, y_ref, acc_ref):
    contrib = _dot(act.astype(bf16), wd_ref[...])

    @pl.when(c == 0)
    def _():
        acc_ref[...] = contrib

    @pl.when(c > 0)
    def _():
        acc_ref[...] += contrib

    @pl.when(c == pl.num_programs(1) - 1)
    def _():
        y_ref[...] = _rmsnorm(x_ref[...] + acc_ref[...], gf_ref[...])


def _ffn_prompt_kernel(h_ref, x_ref, wa_ref, wg_ref, cwa_ref, cwg_ref, pa_ref, pg_ref, wd_ref, gf_ref, y_ref, ta_ref, tg_ref,
                       acc_ref, prev_ref, act_ref, *, tiles_per_seq, n_chunks):
    i, c = pl.program_id(0), pl.program_id(1)
    tm = h_ref.shape[0]
    seq_start = (i % tiles_per_seq) == 0
    row = lax.broadcasted_iota(jnp.int32, (tm, FF_CHUNK), 0)

    def conv_half(h, w_ref, cw_ref, first_ref, slot, tail_ref):
        u = _dot(h, w_ref[...])
        prev = jnp.where(seq_start, first_ref[...], prev_ref[slot])
        u1 = jnp.where(row == 0, prev[7:8, :], pltpu.roll(u, 1, 0))
        u2 = jnp.where(row == 0, prev[6:7, :], jnp.where(row == 1, prev[7:8, :], pltpu.roll(u, 2, 0)))
        cw = cw_ref[...]
        tail = u[tm - 8:, :]
        prev_ref[slot] = tail
        tail_ref[...] = tail
        return cw[2:3, :] * u + cw[1:2, :] * u1 + cw[0:1, :] * u2

    def up():
        h = h_ref[...]
        a = conv_half(h, wa_ref, cwa_ref, pa_ref, c, ta_ref)
        g = conv_half(h, wg_ref, cwg_ref, pg_ref, n_chunks + c, tg_ref)
        act_ref[c % 2] = (_gelu_tanh(a) * g).astype(bf16)

    def down():
        return _dot(act_ref[(c + 1) % 2], wd_ref[...])

    @pl.when(c == 0)
    def _():
        acc_ref[...] = jnp.zeros_like(acc_ref)
        up()

    @pl.when((c > 0) & (c < n_chunks))
    def _():
        acc_ref[...] += down()
        up()

    @pl.when(c == n_chunks)
    def _():
        y_ref[...] = _rmsnorm(x_ref[...] + acc_ref[...] + down(), gf_ref[...])


def _ffn_prompt(h, x1, wup, cw, first_a, first_g, wd, gf, tm, rows_per_seq):
    rows = h.shape[0]
    fc = FF_CHUNK
    nch = D_FF // fc
    nt = rows // tm
    up_c = lambda c: jnp.minimum(c, nch - 1)
    return pl.pallas_call(
        functools.partial(_ffn_prompt_kernel, tiles_per_seq=rows_per_seq // tm, n_chunks=nch),
        out_shape=(jax.ShapeDtypeStruct((rows, D_MODEL), f32),
                   jax.ShapeDtypeStruct((nt, 8, D_FF), f32), jax.ShapeDtypeStruct((nt, 8, D_FF), f32)),
        grid=(nt, nch + 1),
        in_specs=[pl.BlockSpec((tm, D_MODEL), lambda i, c: (i, 0)),
                  pl.BlockSpec((tm, D_MODEL), lambda i, c: (i, 0)),
                  pl.BlockSpec((D_MODEL, fc), lambda i, c: (0, up_c(c))),
                  pl.BlockSpec((D_MODEL, fc), lambda i, c: (0, nch + up_c(c))),
                  pl.BlockSpec((CONV_W, fc), lambda i, c: (0, up_c(c))),
                  pl.BlockSpec((CONV_W, fc), lambda i, c: (0, nch + up_c(c))),
                  pl.BlockSpec((8, fc), lambda i, c: (0, up_c(c))),
                  pl.BlockSpec((8, fc), lambda i, c: (0, up_c(c))),
                  pl.BlockSpec((fc, D_MODEL), lambda i, c: (jnp.maximum(c - 1, 0), 0)),
                  pl.BlockSpec((1, D_MODEL), lambda i, c: (0, 0))],
        out_specs=(pl.BlockSpec((tm, D_MODEL), lambda i, c: (i, 0)),
                   pl.BlockSpec((None, 8, fc), lambda i, c: (i, 0, up_c(c))),
                   pl.BlockSpec((None, 8, fc), lambda i, c: (i, 0, up_c(c)))),
        scratch_shapes=[pltpu.VMEM((tm, D_MODEL), f32), pltpu.VMEM((2 * nch, 8, fc), f32),
                        pltpu.VMEM((2, tm, fc), bf16)],
        compiler_params=_cparams(("arbitrary", "arbitrary")),
        name="conv_ffn_prompt",
    )(h, x1, wup, wup, cw, cw, first_a, first_g, wd, gf)


def _ffn_small_kernel(h_ref, x_ref, wa_ref, wg_ref, cwa_ref, cwg_ref, sta_ref, stg_ref, wd_ref, gf_ref,
                      y_ref, ua_ref, ug_ref, acc_ref, *, n_sample, n_tok):
    c = pl.program_id(1)
    rows = h_ref.shape[0]
    h = h_ref[...]
    row = lax.broadcasted_iota(jnp.int32, (rows, FF_CHUNK), 0)
    t = jnp.where(row < n_sample, row % n_tok, row - n_sample)

    def conv_half(w_ref, cw_ref, st_ref, u_ref):
        u = _dot(h, w_ref[...])
        u_ref[...] = u
        st = st_ref[...]
        u1 = jnp.where(t == 0, pltpu.roll(st, rows - 1, 0), pltpu.roll(u, 1, 0))
        u2 = jnp.where(t < 2, st, pltpu.roll(u, 2, 0))
        cw = cw_ref[...]
        return cw[2:3, :] * u + cw[1:2, :] * u1 + cw[0:1, :] * u2

    a = conv_half(wa_ref, cwa_ref, sta_ref, ua_ref)
    g = conv_half(wg_ref, cwg_ref, stg_ref, ug_ref)
    _ffn_finish(c, _gelu_tanh(a) * g, wd_ref, x_ref, gf_ref, y_ref, acc_ref)


def _ffn_small(h, x1, wup, cw, st, wd, gf, n_sample, n_tok):
    rows = h.shape[0]
    fc = FF_CHUNK
    nch = D_FF // fc
    return pl.pallas_call(
        functools.partial(_ffn_small_kernel, n_sample=n_sample, n_tok=n_tok),
        out_shape=(jax.ShapeDtypeStruct((rows, D_MODEL), f32),
                   jax.ShapeDtypeStruct((rows, D_FF), f32), jax.ShapeDtypeStruct((rows, D_FF), f32)),
        grid=(1, nch),
        in_specs=[pl.BlockSpec((rows, D_MODEL), lambda i, c: (0, 0)),
                  pl.BlockSpec((rows, D_MODEL), lambda i, c: (0, 0)),
                  pl.BlockSpec((D_MODEL, fc), lambda i, c: (0, c)),
                  pl.BlockSpec((D_MODEL, fc), lambda i, c: (0, nch + c)),
                  pl.BlockSpec((CONV_W, fc), lambda i, c: (0, c)),
                  pl.BlockSpec((CONV_W, fc), lambda i, c: (0, nch + c)),
                  pl.BlockSpec((rows, fc), lambda i, c: (0, c)),
                  pl.BlockSpec((rows, fc), lambda i, c: (0, nch + c)),
                  pl.BlockSpec((fc, D_MODEL), lambda i, c: (c, 0)),
                  pl.BlockSpec((1, D_MODEL), lambda i, c: (0, 0))],
        out_specs=(pl.BlockSpec((rows, D_MODEL), lambda i, c: (0, 0)),
                   pl.BlockSpec((rows, fc), lambda i, c: (0, c)),
                   pl.BlockSpec((rows, fc), lambda i, c: (0, c))),
        scratch_shapes=[pltpu.VMEM((rows, D_MODEL), f32)],
        compiler_params=_cparams(("arbitrary", "arbitrary")),
        name="conv_ffn_small",
    )(h, x1, wup, wup, cw, cw, st, st, wd, gf)


def _rope_tables(pos):
    half = RET_DK // 2
    inv = ROPE_BASE ** (-jnp.arange(half, dtype=f32) / half)
    ang = pos.astype(f32)[:, None] * inv[None, :]
    c, s = jnp.cos(ang), jnp.sin(ang)
    return jnp.concatenate([c, c], axis=-1), jnp.concatenate([-s, s], axis=-1)


def _layer(xp, xs, ck, cv, page_table, s_ret, s_conv, meta, norm1_g, w_in, sb_bias, w_ret_proj, w_sb_proj, w_o,
           norm2_g, w_up, conv_w, w_down, norm_f_g):
    batch, seq, d = xp.shape
    nb, n_tok, _ = xs.shape
    n_pool, page = ck.shape[0], ck.shape[1]
    n_pages = page_table.shape[1]
    past = n_pages * page
    n_sample = nb * n_tok
    n_small = n_sample + N_META

    w_in, w_ret_proj, w_sb_proj, w_o, w_up, w_down = (
        w.astype(bf16) for w in (w_in, w_ret_proj, w_sb_proj, w_o, w_up, w_down))
    g1, g2, gf = norm1_g[None, :], norm2_g[None, :], norm_f_g[None, :]
    x_big = xp.reshape(batch * seq, d)
    x_small = jnp.concatenate([xs.reshape(n_sample, d), meta.astype(f32)], axis=0)

    tm_big = 1024
    proj_s = _norm_mm(x_small, g1, w_in, n_small, 1024)
    proj_p = _norm_mm(x_big, g1, w_in, tm_big, 1024)

    lg = jnp.log1p(-jnp.exp2(-5.0 - jnp.arange(RET_HEADS, dtype=f32)))
    lg = jnp.broadcast_to(lg[:, None, None], (RET_HEADS, 1, RET_DV))
    cos_m, sin_m = _rope_tables(jnp.arange(N_META))
    ret_m, s_meta = _ret_small(proj_s, n_sample, N_META, N_META, cos_m, sin_m, lg,
                               jnp.zeros((1, RET_HEADS, RET_DK, RET_DV), f32))
    cos_p, sin_p = _rope_tables(N_META + jnp.arange(seq))
    ret_p, sret_p = _ret_prompt(proj_p, cos_p, sin_p, lg, s_meta[0], batch, seq)
    cos_s, sin_s = _rope_tables(past + (jnp.arange(SMALL_ROWS) % n_tok))
    ret_s, sret_s = _ret_small(proj_s, 0, n_sample, n_tok, cos_s, sin_s, lg, s_ret)

    tri = (jnp.arange(SB_BK)[:, None] > jnp.arange(SB_BK)[None, :]).astype(bf16)
    pad_rows = lambda a, n: jnp.pad(a, [(0, 0)] * (a.ndim - 2) + [(0, n - a.shape[-2]), (0, 0)])
    km = pad_rows(proj_s[n_sample:, C_SK:C_SK + SB_W], 128).astype(bf16)
    vm = pad_rows(proj_s[n_sample:, C_SV:C_SV + SB_W], 128).astype(bf16)
    sb_m = _sb_meta(proj_s, n_sample, km, vm, sb_bias, tri)
    ps3 = proj_s[:n_sample].reshape(nb, n_tok, IN_WIDTH)
    cache_t = lambda c: jnp.transpose(c, (0, 2, 3, 1)).reshape(n_pool, SB_W, page)
    bias_col = jnp.tile(sb_bias * LOG2E, n_tok)[:, None]
    sb_p, sb_s = _stick_breaking(proj_p, km, vm, sb_bias, tri, batch, seq,
                                 page_table, ps3, cache_t(ck), cache_t(cv), bias_col)

    ret_small = jnp.concatenate([ret_s, ret_m], axis=0)
    sb_small = jnp.concatenate([sb_s.reshape(n_sample, SB_W).astype(bf16), sb_m], axis=0)
    m_s = _gate_mix(ret_small, sb_small, proj_s, w_ret_proj, w_sb_proj, n_small, 512)
    m_p = _gate_mix(ret_p, sb_p, proj_p, w_ret_proj, w_sb_proj, tm_big, 512)
    x1_s, h2_s = _out_proj(x_small, m_s, w_o, g2, n_small)
    x1_p, h2_p = _out_proj(x_big, m_p, w_o, g2, 512)

    st = jnp.concatenate([jnp.pad(s_conv, ((0, 0), (0, n_tok - (CONV_W - 1)), (0, 0))).reshape(n_sample, 2 * D_FF),
                          jnp.zeros((N_META, 2 * D_FF), f32)], axis=0)
    y_s, ua_s, ug_s = _ffn_small(h2_s, x1_s, w_up, conv_w, st, w_down, gf, n_sample, n_tok)
    y_p, ta_p, tg_p = _ffn_prompt(h2_p, x1_p, w_up, conv_w, ua_s[n_small - 8:], ug_s[n_small - 8:], w_down, gf, 512, seq)

    y_prompt = y_p.reshape(batch, seq, d)
    y_sample = y_s[:n_sample].reshape(nb, n_tok, d)

    kt, vt = _kv_out(proj_p, proj_s, n_sample, batch, seq)
    token_major = lambda a: jnp.transpose(a.reshape(batch, SB_HEADS, SB_DH, N_META + seq), (0, 3, 1, 2))
    k_p, v_p = token_major(kt), token_major(vt)
    k_s = ps3[:, :, C_SK:C_SK + SB_W].reshape(nb, n_tok, SB_HEADS, SB_DH)
    v_s = ps3[:, :, C_SV:C_SV + SB_W].reshape(nb, n_tok, SB_HEADS, SB_DH)
    tiles_per_seq = seq // 512
    last = lambda tl: tl.reshape(batch, tiles_per_seq, 8, D_FF)[:, -1, 8 - (CONV_W - 1):, :]
    conv_p = jnp.concatenate([last(ta_p), last(tg_p)], axis=-1)
    u_s = jnp.concatenate([ua_s[:n_sample], ug_s[:n_sample]], axis=-1).reshape(nb, n_tok, 2 * D_FF)
    conv_s = u_s[:, n_tok - (CONV_W - 1):, :]
    return y_prompt, y_sample, k_p, v_p, k_s, v_s, sret_p, sret_s, conv_p, conv_s


def kernel(x_prompt, x_sample, cache_k, cache_v, page_table, state_ret, state_conv, meta_tokens, norm1_g, w_in,
           sb_bias, w_ret_proj, w_sb_proj, w_o, norm2_g, w_up, conv_w, w_down, norm_f_g):
    assert cache_k.shape[0] == 1, "one layer"
    outs = _layer(x_prompt, x_sample, cache_k[0], cache_v[0], page_table, state_ret[0], state_conv[0], meta_tokens,
                  norm1_g[0], w_in[0], sb_bias[0], w_ret_proj[0], w_sb_proj[0], w_o[0], norm2_g[0], w_up[0],
                  conv_w[0], w_down[0], norm_f_g)
    y_prompt, y_sample = outs[0], outs[1]
    return (y_prompt, y_sample) + tuple(o[None] for o in outs[2:])
```

```python
import functools

import jax
import jax.numpy as jnp
from jax import lax
from jax.experimental import pallas as pl
from jax.experimental.pallas import tpu as pltpu

f32 = jnp.float32
bf16 = jnp.bfloat16

D_MODEL = 2048
N_META = 16
RET_HEADS = 8
RET_DK = 128
RET_DV = 256
SB_HEADS = 16
SB_DH = 64
D_FF = 5632
CONV_W = 3
ROPE_BASE = 10000.0
EPS = 1e-6
GN_EPS = 1e-5

RET_QK = RET_HEADS * RET_DK
RET_V = RET_HEADS * RET_DV
SB_W = SB_HEADS * SB_DH
C_RQ, C_RK, C_RV, C_RG = 0, RET_QK, 2 * RET_QK, 2 * RET_QK + RET_V
C_SQ = C_RG + RET_V
C_SK, C_SV = C_SQ + SB_W, C_SQ + 2 * SB_W
C_GR = C_SV + SB_W
C_GS = C_GR + D_MODEL
IN_WIDTH = C_GS + D_MODEL

RET_CHUNK = 128
RET_HEAD_GROUP = 8
SB_BQ = 512
SB_BK = 256
LOG2E = 1.4426950408889634
SB_LOGIT_CLAMP = 100.0
SB_HEAD_GROUP = 4
SMALL_ROWS = 16
FF_CHUNK = 512
VMEM_LIMIT = 56 * 1024 * 1024


def _cparams(sem):
    return pltpu.CompilerParams(dimension_semantics=sem, vmem_limit_bytes=VMEM_LIMIT)


def _dot(a, b):
    return jnp.dot(a, b, preferred_element_type=f32)


def _dot_nt(a, b):
    return lax.dot_general(a, b, (((1,), (1,)), ((), ())), preferred_element_type=f32)


def _sigmoid(x):
    return 1.0 / (1.0 + jnp.exp(-x))


def _rmsnorm(x, g):
    return x * lax.rsqrt(jnp.mean(x * x, axis=-1, keepdims=True) + EPS) * g


def _norm_mm_kernel(x_ref, g_ref, w_ref, o_ref, h_ref):
    @pl.when(pl.program_id(1) == 0)
    def _():
        h_ref[...] = _rmsnorm(x_ref[...], g_ref[...]).astype(bf16)

    o_ref[...] = _dot(h_ref[...], w_ref[...])


def _norm_mm(x, g, w, tm, tn):
    m, d = x.shape
    n = w.shape[1]
    return pl.pallas_call(
        _norm_mm_kernel,
        out_shape=jax.ShapeDtypeStruct((m, n), f32),
        grid=(m // tm, n // tn),
        in_specs=[pl.BlockSpec((tm, d), lambda i, j: (i, 0)),
                  pl.BlockSpec((1, d), lambda i, j: (0, 0)),
                  pl.BlockSpec((d, tn), lambda i, j: (0, j))],
        out_specs=pl.BlockSpec((tm, tn), lambda i, j: (i, j)),
        scratch_shapes=[pltpu.VMEM((tm, d), bf16)],
        compiler_params=_cparams(("parallel", "arbitrary")),
        name="norm_in_proj",
    )(x, g, w)


def _rope(x, cos2, sin2):
    return x * cos2 + pltpu.roll(x, RET_DK // 2, 1) * sin2


def _gated_groupnorm(o, gate):
    mu = jnp.mean(o, axis=-1, keepdims=True)
    oc = o - mu
    var = jnp.mean(oc * oc, axis=-1, keepdims=True)
    return (gate * _sigmoid(gate)) * (oc * lax.rsqrt(var + GN_EPS))


def _ret_prompt_kernel(q_ref, k_ref, v_ref, g_ref, cos_ref, sin_ref, lg_ref, s0_ref, o_ref, sfin_ref, s_ref):
    c = pl.program_id(2)
    n = RET_CHUNK

    @pl.when(c == 0)
    def _():
        s_ref[...] = s0_ref[...]

    row = lax.broadcasted_iota(jnp.int32, (n, RET_DK), 0).astype(f32)
    col = lax.broadcasted_iota(jnp.int32, (n, n), 1).astype(f32)
    diff = row - col
    cos2, sin2 = cos_ref[...], sin_ref[...]
    for h in range(RET_HEAD_GROUP):
        dk = slice(h * RET_DK, (h + 1) * RET_DK)
        dv = slice(h * RET_DV, (h + 1) * RET_DV)
        lgv = lg_ref[h]
        lg = lgv[:, :RET_DK]
        q = _rope(q_ref[:, dk], cos2, sin2)
        k = _rope(k_ref[:, dk], cos2, sin2) * (RET_DK ** -0.5)
        decay = jnp.where(diff >= 0, jnp.exp(diff * lg), 0.0)
        att = _dot_nt(q.astype(bf16), k.astype(bf16)) * decay
        vb = v_ref[:, dv].astype(bf16)
        s = s_ref[h]
        o = _dot(att.astype(bf16), vb) + _dot((q * jnp.exp((row + 1.0) * lg)).astype(bf16), s.astype(bf16))
        kd = k * jnp.exp((n - 1.0 - row) * lg)
        s_ref[h] = jnp.exp(n * lgv) * s + _dot(kd.T.astype(bf16), vb)
        o_ref[:, dv] = _gated_groupnorm(o, g_ref[:, dv]).astype(bf16)

    @pl.when(c == pl.num_programs(2) - 1)
    def _():
        sfin_ref[...] = s_ref[...]


def _ret_prompt(proj, cos2, sin2, lg, s0, batch, seq):
    nc = seq // RET_CHUNK
    n = RET_CHUNK
    hg = RET_HEAD_GROUP
    wk, wv = hg * RET_DK, hg * RET_DV
    row = lambda b, h, c: b * nc + c
    return pl.pallas_call(
        _ret_prompt_kernel,
        out_shape=(jax.ShapeDtypeStruct((batch * seq, RET_V), bf16),
                   jax.ShapeDtypeStruct((batch, RET_HEADS, RET_DK, RET_DV), f32)),
        grid=(batch, RET_HEADS // hg, nc),
        in_specs=[pl.BlockSpec((n, wk), lambda b, h, c: (row(b, h, c), C_RQ // wk + h)),
                  pl.BlockSpec((n, wk), lambda b, h, c: (row(b, h, c), C_RK // wk + h)),
                  pl.BlockSpec((n, wv), lambda b, h, c: (row(b, h, c), C_RV // wv + h)),
                  pl.BlockSpec((n, wv), lambda b, h, c: (row(b, h, c), C_RG // wv + h)),
                  pl.BlockSpec((n, RET_DK), lambda b, h, c: (c, 0)),
                  pl.BlockSpec((n, RET_DK), lambda b, h, c: (c, 0)),
                  pl.BlockSpec((hg, 1, RET_DV), lambda b, h, c: (h, 0, 0)),
                  pl.BlockSpec((hg, RET_DK, RET_DV), lambda b, h, c: (h, 0, 0))],
        out_specs=(pl.BlockSpec((n, wv), lambda b, h, c: (row(b, h, c), h)),
                   pl.BlockSpec((None, hg, RET_DK, RET_DV), lambda b, h, c: (b, h, 0, 0))),
        scratch_shapes=[pltpu.VMEM((hg, RET_DK, RET_DV), f32)],
        compiler_params=_cparams(("parallel", "parallel", "arbitrary")),
        name="retention_prompt",
    )(proj, proj, proj, proj, cos2, sin2, lg, s0)


def _ret_small_kernel(q_ref, k_ref, v_ref, g_ref, cos_ref, sin_ref, lg_ref, s0_ref, o_ref, s_ref, *, n_tok):
    r = SMALL_ROWS
    n_seq = r // n_tok
    ri = lax.broadcasted_iota(jnp.int32, (r, RET_DK), 0)
    ti = (ri % n_tok).astype(f32)
    rj = lax.broadcasted_iota(jnp.int32, (r, 128), 1)
    rr = lax.broadcasted_iota(jnp.int32, (r, 128), 0)
    tdiff = ((rr % n_tok) - (rj % n_tok)).astype(f32)
    visible = ((rr // n_tok) == (rj // n_tok)) & (tdiff >= 0)
    rk = lax.broadcasted_iota(jnp.int32, (128, RET_DK), 0)
    tk = (rk % n_tok).astype(f32)
    rv = lax.broadcasted_iota(jnp.int32, (r, RET_DV), 0)
    cos2, sin2 = cos_ref[...], sin_ref[...]
    pad = 128 - r
    for h in range(RET_HEADS):
        lgv = lg_ref[h]
        lg = lgv[:, :RET_DK]
        q = _rope(q_ref[:, h * RET_DK:(h + 1) * RET_DK], cos2, sin2)
        k = _rope(k_ref[:, h * RET_DK:(h + 1) * RET_DK], cos2, sin2) * (RET_DK ** -0.5)
        kpad = jnp.concatenate([k, jnp.zeros((pad, RET_DK), f32)], axis=0)
        vpad = jnp.concatenate([v_ref[:, h * RET_DV:(h + 1) * RET_DV], jnp.zeros((pad, RET_DV), f32)],
                               axis=0).astype(bf16)
        decay = jnp.where(visible, jnp.exp(tdiff * lg), 0.0)
        att = _dot_nt(q.astype(bf16), kpad.astype(bf16)) * decay
        o = _dot(att.astype(bf16), vpad)
        qd = (q * jnp.exp((ti + 1.0) * lg)).astype(bf16)
        kd = kpad * jnp.exp((n_tok - 1.0 - tk) * lg)
        for s_i in range(n_seq):
            s = s0_ref[s_i, h]
            mine_k = (rk // n_tok) == s_i
            mine_v = (rv // n_tok) == s_i
            o = o + jnp.where(mine_v, _dot(qd, s.astype(bf16)), 0.0)
            s_ref[s_i, h] = jnp.exp(n_tok * lgv) * s + _dot(jnp.where(mine_k, kd, 0.0).T.astype(bf16), vpad)
        o_ref[:, h * RET_DV:(h + 1) * RET_DV] = _gated_groupnorm(
            o, g_ref[:, h * RET_DV:(h + 1) * RET_DV]).astype(bf16)


def _ret_small(proj, row0, n_rows, n_tok, cos2, sin2, lg, s0):
    r = SMALL_ROWS
    n_seq = r // n_tok
    steps = n_rows // r
    b0 = row0 // r
    return pl.pallas_call(
        functools.partial(_ret_small_kernel, n_tok=n_tok),
        out_shape=(jax.ShapeDtypeStruct((n_rows, RET_V), bf16),
                   jax.ShapeDtypeStruct(s0.shape, f32)),
        grid=(steps,),
        in_specs=[pl.BlockSpec((r, RET_QK), lambda i: (b0 + i, C_RQ // RET_QK)),
                  pl.BlockSpec((r, RET_QK), lambda i: (b0 + i, C_RK // RET_QK)),
                  pl.BlockSpec((r, RET_V), lambda i: (b0 + i, C_RV // RET_V)),
                  pl.BlockSpec((r, RET_V), lambda i: (b0 + i, C_RG // RET_V)),
                  pl.BlockSpec((r, RET_DK), lambda i: (0, 0)),
                  pl.BlockSpec((r, RET_DK), lambda i: (0, 0)),
                  pl.BlockSpec((RET_HEADS, 1, RET_DV), lambda i: (0, 0, 0)),
                  pl.BlockSpec((n_seq, RET_HEADS, RET_DK, RET_DV), lambda i: (i, 0, 0, 0))],
        out_specs=(pl.BlockSpec((r, RET_V), lambda i: (i, 0)),
                   pl.BlockSpec((n_seq, RET_HEADS, RET_DK, RET_DV), lambda i: (i, 0, 0, 0))),
        compiler_params=_cparams(("parallel",)),
        name="retention_small",
    )(proj, proj, proj, proj, cos2, sin2, lg, s0)


def _sb_log2_keep(z2, mask):
    zc = jnp.minimum(z2, SB_LOGIT_CLAMP)
    lk = jnp.log(1.0 + jnp.exp2(zc)) * -LOG2E
    return zc, (lk if mask is None else jnp.where(mask, lk, 0.0))


def _sb_terms(z2, tri, mask):
    zc, lk = _sb_log2_keep(z2, mask)
    between = _dot(lk.astype(bf16), tri)
    w = jnp.exp2(lk + between + zc)
    if mask is not None:
        w = jnp.where(mask, w, 0.0)
    return w.astype(bf16), between[:, :1] + lk[:, :1]


def _sb_kernel(pt_ref, bias_ref, q_ref, k_ref, v_ref, km_ref, vm_ref, tri_ref, qs_ref, kn_ref, vn_ref, *rest,
               n_tok, n_grp):
    k_refs, v_refs = rest[:n_grp], rest[n_grp:2 * n_grp]
    bcol_ref, o_ref, os_ref, acc_ref, r_ref, mc_ref, mt_ref, qg_ref, accs_ref, rs_ref, new_ref = rest[2 * n_grp:]
    p, i = pl.program_id(1), pl.program_id(2)
    bq, bk = SB_BQ, SB_BK
    tri = tri_ref[...]

    page = k_refs[0].shape[1]
    rows = n_tok * SB_HEADS
    grp_rows = n_tok * SB_HEAD_GROUP
    grp_cols = SB_DH * SB_HEAD_GROUP
    n_row_grp = SB_HEADS // SB_HEAD_GROUP
    own = (lax.broadcasted_iota(jnp.int32, (grp_rows, grp_cols), 1) // SB_DH
           == lax.broadcasted_iota(jnp.int32, (grp_rows, grp_cols), 0) // n_tok)
    tri_p = tri[:page, :page]
    row_grp = lambda a, g: a[g * grp_rows:(g + 1) * grp_rows]
    col_grp = lambda g: slice(g * grp_cols, (g + 1) * grp_cols)
    by_row_grp = lambda f: jnp.concatenate([f(g) for g in range(n_row_grp)], axis=0)

    def sample_blocks(logits, weigh, mask, init):
        zc, lk = zip(*[_sb_log2_keep(z2 + bcol_ref[...], mask) for z2 in logits])
        between = _dot(jnp.concatenate([l.astype(bf16) for l in lk], axis=0), tri_p)
        r = jnp.zeros((rows, 1), f32) if init else rs_ref[...]
        acc = None if init else accs_ref[...]
        for g in range(len(logits)):
            btw = between[g * rows:(g + 1) * rows]
            w = jnp.exp2(lk[g] + btw + r + zc[g])
            if mask is not None:
                w = jnp.where(mask, w, 0.0)
            r = r + btw[:, :1] + lk[g][:, :1]
            contrib = weigh[g](w.astype(bf16))
            acc = contrib if acc is None else acc + contrib
        rs_ref[...] = r
        accs_ref[...] = acc

    @pl.when(i == 0)
    def _():
        qs = qs_ref[...] * (SB_DH ** -0.5 * LOG2E)
        for h in range(SB_HEADS):
            g = h // SB_HEAD_GROUP
            qg_ref[h * n_tok:(h + 1) * n_tok, :] = jnp.where(
                own[(h % SB_HEAD_GROUP) * n_tok:(h % SB_HEAD_GROUP + 1) * n_tok], qs[:, col_grp(g)], 0.0)
        new_ref[...] = jnp.zeros_like(new_ref)
        new_ref[0, :n_tok, :] = kn_ref[...]
        new_ref[1, :n_tok, :] = vn_ref[...]
        qg = qg_ref[...].astype(bf16)
        kk = lax.broadcasted_iota(jnp.int32, (rows, page), 1)
        tq = lax.broadcasted_iota(jnp.int32, (rows, page), 0) % n_tok
        z_new = by_row_grp(lambda g: _dot_nt(row_grp(qg, g), new_ref[0, :, col_grp(g)].astype(bf16)))
        wv_new = lambda w: by_row_grp(lambda g: _dot(row_grp(w, g), new_ref[1, :, col_grp(g)].astype(bf16)))
        sample_blocks([z_new], [wv_new], kk < tq, True)

    qg = qg_ref[...].astype(bf16)
    sample_blocks(
        [by_row_grp(lambda g, r_=r_: _dot(row_grp(qg, g), r_[col_grp(g), :].astype(bf16))) for r_ in k_refs],
        [lambda w, r_=r_: by_row_grp(lambda g: _dot_nt(row_grp(w, g), r_[col_grp(g), :].astype(bf16)))
         for r_ in v_refs], None, False)

    q = q_ref[...] * (SB_DH ** -0.5 * LOG2E)
    lane = lax.broadcasted_iota(jnp.int32, (bq, 2 * SB_DH), 1)
    first = lane < SB_DH
    qm = (jnp.where(first, q, 0.0).astype(bf16), jnp.where(first, 0.0, q).astype(bf16))
    bias = (bias_ref[2 * p] * LOG2E, bias_ref[2 * p + 1] * LOG2E)

    def terms(kb, vb, mask, tri_b):
        out = []
        for hh in range(2):
            w, total = _sb_terms(_dot_nt(qm[hh], kb) + bias[hh], tri_b, mask)
            out.append((_dot(w, vb), total))
        return out

    def accumulate(res):
        for hh, (contrib, total) in enumerate(res):
            r = r_ref[hh]
            acc_ref[hh] += jnp.exp2(r) * contrib
            r_ref[hh] = r + total

    def kv(j):
        s0 = pl.multiple_of(j * bk, bk)
        return k_ref[pl.ds(s0, bk), :].astype(bf16), v_ref[pl.ds(s0, bk), :].astype(bf16)

    nkb = bq // bk
    kk = lax.broadcasted_iota(jnp.int32, (bq, bk), 1)
    qq = lax.broadcasted_iota(jnp.int32, (bq, bk), 0)
    for d in reversed(range(nkb)):
        res = terms(*kv(i * nkb + d), kk + d * bk < qq, tri)
        if d == nkb - 1:
            for hh, (contrib, total) in enumerate(res):
                acc_ref[hh] = contrib
                r_ref[hh] = total
        else:
            accumulate(res)

    nm = km_ref.shape[0]
    km_valid = lax.broadcasted_iota(jnp.int32, (bq, nm), 1) < N_META
    for hh, (contrib, total) in enumerate(terms(km_ref[...], vm_ref[...], km_valid, tri[:nm, :nm])):
        mc_ref[hh] = contrib
        mt_ref[hh] = total

    def pair(t, carry):
        newer = terms(*kv(i * nkb - 1 - 2 * t), None, tri)
        older = terms(*kv(i * nkb - 2 - 2 * t), None, tri)
        accumulate(newer)
        accumulate(older)
        return carry

    lax.fori_loop(0, i * (nkb // 2), pair, 0)

    accumulate([(mc_ref[hh], mt_ref[hh]) for hh in range(2)])
    o_ref[...] = jnp.where(first, acc_ref[0], acc_ref[1]).astype(bf16)

    @pl.when(i == pl.num_programs(2) - 1)
    def _():
        for g in range(n_row_grp):
            sel = jnp.where(own, row_grp(accs_ref[...], g), 0.0)
            os_ref[:, col_grp(g)] = sum(sel[hh * n_tok:(hh + 1) * n_tok] for hh in range(SB_HEAD_GROUP))


def _stick_breaking(proj, km, vm, bias, tri, batch, seq, page_table, proj3, ck, cv, bias_col):
    nq = seq // SB_BQ
    hp = 2 * SB_DH
    n_pairs = SB_HEADS // 2
    nb, n_tok, _ = proj3.shape
    n_pages = page_table.shape[1]
    page = ck.shape[2]
    rows = n_tok * SB_HEADS
    assert nb == batch * n_pairs and n_pages % nq == 0
    n_grp = n_pages // nq

    def page_spec(g):
        return pl.BlockSpec((None, SB_W, page),
                            lambda b, p, i, pt, bs: (pt[b * n_pairs + p, n_pages - 1 - (i * n_grp + g)], 0, 0))

    new_tok = lambda col: pl.BlockSpec((None, n_tok, SB_W), lambda b, p, i, pt, bs: (b * n_pairs + p, 0, col // SB_W))
    return pl.pallas_call(
        functools.partial(_sb_kernel, n_tok=n_tok, n_grp=n_grp),
        out_shape=(jax.ShapeDtypeStruct((batch * seq, SB_W), bf16), jax.ShapeDtypeStruct((nb, n_tok, SB_W), f32)),
        grid_spec=pltpu.PrefetchScalarGridSpec(
            num_scalar_prefetch=2,
            grid=(batch, n_pairs, nq),
            in_specs=([pl.BlockSpec((SB_BQ, hp), lambda b, p, i, pt, bs: (b * nq + i, C_SQ // hp + p)),
                       pl.BlockSpec((seq, hp), lambda b, p, i, pt, bs: (b, C_SK // hp + p)),
                       pl.BlockSpec((seq, hp), lambda b, p, i, pt, bs: (b, C_SV // hp + p)),
                       pl.BlockSpec((km.shape[0], hp), lambda b, p, i, pt, bs: (0, p)),
                       pl.BlockSpec((km.shape[0], hp), lambda b, p, i, pt, bs: (0, p)),
                       pl.BlockSpec((SB_BK, SB_BK), lambda b, p, i, pt, bs: (0, 0)),
                       new_tok(C_SQ), new_tok(C_SK), new_tok(C_SV)]
                      + [page_spec(g) for g in range(n_grp)] + [page_spec(g) for g in range(n_grp)]
                      + [pl.BlockSpec((rows, 1), lambda b, p, i, pt, bs: (0, 0))]),
            out_specs=(pl.BlockSpec((SB_BQ, hp), lambda b, p, i, pt, bs: (b * nq + i, p)),
                       new_tok(0)),
            scratch_shapes=[pltpu.VMEM((2, SB_BQ, hp), f32), pltpu.VMEM((2, SB_BQ, 1), f32),
                            pltpu.VMEM((2, SB_BQ, hp), f32), pltpu.VMEM((2, SB_BQ, 1), f32),
                            pltpu.VMEM((rows, SB_DH * SB_HEAD_GROUP), f32),
                            pltpu.VMEM((rows, SB_DH * SB_HEAD_GROUP), f32),
                            pltpu.VMEM((rows, 1), f32), pltpu.VMEM((2, page, SB_W), f32)]),
        compiler_params=_cparams(("arbitrary", "arbitrary", "arbitrary")),
        name="stick_breaking",
    )(page_table, bias, proj, proj, proj, km, vm, tri, proj3, proj3, proj3,
      *([ck] * n_grp), *([cv] * n_grp), bias_col)


def _sb_meta_kernel(bias_ref, q_ref, km_ref, vm_ref, tri_ref, o_ref):
    p = pl.program_id(0)
    nq, nm = q_ref.shape[0], km_ref.shape[0]
    q = q_ref[...] * (SB_DH ** -0.5 * LOG2E)
    first = lax.broadcasted_iota(jnp.int32, (nq, 2 * SB_DH), 1) < SB_DH
    mask = lax.broadcasted_iota(jnp.int32, (nq, nm), 1) < lax.broadcasted_iota(jnp.int32, (nq, nm), 0)
    tri = tri_ref[...][:nm, :nm]
    kb, vb = km_ref[...], vm_ref[...]
    out = []
    for hh, qh in enumerate((jnp.where(first, q, 0.0), jnp.where(first, 0.0, q))):
        z2 = _dot_nt(qh.astype(bf16), kb) + bias_ref[2 * p + hh] * LOG2E
        w, _ = _sb_terms(z2, tri, mask)
        out.append(_dot(w, vb))
    o_ref[...] = jnp.where(first, out[0], out[1]).astype(bf16)


def _sb_meta(proj_small, row0, km, vm, bias, tri):
    hp = 2 * SB_DH
    nm = km.shape[0]
    return pl.pallas_call(
        _sb_meta_kernel,
        out_shape=jax.ShapeDtypeStruct((N_META, SB_W), bf16),
        grid_spec=pltpu.PrefetchScalarGridSpec(
            num_scalar_prefetch=1,
            grid=(SB_HEADS // 2,),
            in_specs=[pl.BlockSpec((N_META, hp), lambda p, s: (row0 // N_META, C_SQ // hp + p)),
                      pl.BlockSpec((nm, hp), lambda p, s: (0, p)),
                      pl.BlockSpec((nm, hp), lambda p, s: (0, p)),
                      pl.BlockSpec((SB_BK, SB_BK), lambda p, s: (0, 0))],
            out_specs=pl.BlockSpec((N_META, hp), lambda p, s: (0, p))),
        compiler_params=_cparams(("parallel",)),
        name="stick_breaking_meta",
    )(bias, proj_small, km, vm, tri)


def _kv_out_kernel(mk_ref, k_ref, mv_ref, v_ref, ok_ref, ov_ref):
    seq = k_ref.shape[0]
    lanes = k_ref.shape[1]
    for m_ref, x_ref, o_ref in ((mk_ref, k_ref, ok_ref), (mv_ref, v_ref, ov_ref)):
        o_ref[:, 0:lanes] = jnp.concatenate([m_ref[...], x_ref[0:lanes - N_META, :]], axis=0).T
        for t in range(1, seq // lanes):
            o_ref[:, t * lanes:(t + 1) * lanes] = x_ref[t * lanes - N_META:(t + 1) * lanes - N_META, :].T
        last = jnp.concatenate([x_ref[seq - N_META:seq, :], jnp.zeros((lanes - N_META, lanes), f32)], axis=0)
        o_ref[:, seq:seq + N_META] = last.T[:, :N_META]


def _kv_out(proj, proj_small, meta_row0, batch, seq):
    lanes = 128
    total = N_META + seq
    meta = lambda col: pl.BlockSpec((N_META, lanes), lambda b, c: (meta_row0 // N_META, col // lanes + c))
    rows = lambda col: pl.BlockSpec((seq, lanes), lambda b, c: (b, col // lanes + c))
    out = pl.BlockSpec((None, lanes, total), lambda b, c: (b, c, 0))
    return pl.pallas_call(
        _kv_out_kernel,
        out_shape=(jax.ShapeDtypeStruct((batch, SB_W, total), f32), jax.ShapeDtypeStruct((batch, SB_W, total), f32)),
        grid=(batch, SB_W // lanes),
        in_specs=[meta(C_SK), rows(C_SK), meta(C_SV), rows(C_SV)],
        out_specs=(out, out),
        compiler_params=_cparams(("parallel", "parallel")),
        name="kv_prompt_out",
    )(proj_small, proj, proj_small, proj)


def _gate_mix_kernel(ret_ref, sb_ref, gr_ref, gs_ref, wr_ref, ws_ref, o_ref):
    m = _sigmoid(gr_ref[...]) * _dot(ret_ref[...], wr_ref[...]) + _sigmoid(gs_ref[...]) * _dot(sb_ref[...], ws_ref[...])
    o_ref[...] = m.astype(bf16)


def _gate_mix(ret, sb, proj, wr, ws, tm, tn):
    m = ret.shape[0]
    return pl.pallas_call(
        _gate_mix_kernel,
        out_shape=jax.ShapeDtypeStruct((m, D_MODEL), bf16),
        grid=(m // tm, D_MODEL // tn),
        in_specs=[pl.BlockSpec((tm, RET_V), lambda i, j: (i, 0)),
                  pl.BlockSpec((tm, SB_W), lambda i, j: (i, 0)),
                  pl.BlockSpec((tm, tn), lambda i, j: (i, C_GR // tn + j)),
                  pl.BlockSpec((tm, tn), lambda i, j: (i, C_GS // tn + j)),
                  pl.BlockSpec((RET_V, tn), lambda i, j: (0, j)),
                  pl.BlockSpec((SB_W, tn), lambda i, j: (0, j))],
        out_specs=pl.BlockSpec((tm, tn), lambda i, j: (i, j)),
        compiler_params=_cparams(("parallel", "arbitrary")),
        name="gate_mix",
    )(ret, sb, proj, proj, wr, ws)


def _out_proj_kernel(x_ref, m_ref, wo_ref, g_ref, x1_ref, h_ref):
    x1 = x_ref[...] + _dot(m_ref[...], wo_ref[...])
    x1_ref[...] = x1
    h_ref[...] = _rmsnorm(x1, g_ref[...]).astype(bf16)


def _out_proj(x, m, wo, g, tm):
    rows = x.shape[0]
    return pl.pallas_call(
        _out_proj_kernel,
        out_shape=(jax.ShapeDtypeStruct((rows, D_MODEL), f32), jax.ShapeDtypeStruct((rows, D_MODEL), bf16)),
        grid=(rows // tm,),
        in_specs=[pl.BlockSpec((tm, D_MODEL), lambda i: (i, 0)),
                  pl.BlockSpec((tm, D_MODEL), lambda i: (i, 0)),
                  pl.BlockSpec((D_MODEL, D_MODEL), lambda i: (0, 0)),
                  pl.BlockSpec((1, D_MODEL), lambda i: (0, 0))],
        out_specs=(pl.BlockSpec((tm, D_MODEL), lambda i: (i, 0)), pl.BlockSpec((tm, D_MODEL), lambda i: (i, 0))),
        compiler_params=_cparams(("parallel",)),
        name="out_proj_norm",
    )(x, m, wo, g)


def _gelu_tanh(x):
    return 0.5 * x * (1.0 + jnp.tanh(0.7978845608028654 * (x + 0.044715 * (x * x * x))))


def _ffn_finish(c, act, wd_ref, x_ref, gf_ref, y_ref, acc_ref):
    contrib = _dot(act.astype(bf16), wd_ref[...])

    @pl.when(c == 0)
    def _():
        acc_ref[...] = contrib

    @pl.when(c > 0)
    def _():
        acc_ref[...] += contrib

    @pl.when(c == pl.num_programs(1) - 1)
    def _():
        y_ref[...] = _rmsnorm(x_ref[...] + acc_ref[...], gf_ref[...])


def _ffn_prompt_kernel(h_ref, x_ref, wa_ref, wg_ref, cwa_ref, cwg_ref, pa_ref, pg_ref, wd_ref, gf_ref, y_ref, ta_ref, tg_ref,
                       acc_ref, prev_ref, act_ref, *, tiles_per_seq, n_chunks):
    i, c = pl.program_id(0), pl.program_id(1)
    tm = h_ref.shape[0]
    seq_start = (i % tiles_per_seq) == 0
    row = lax.broadcasted_iota(jnp.int32, (8, FF_CHUNK), 0)

    def conv_half(u, cw_ref, first_ref, slot, tail_ref):
        prev = jnp.where(seq_start, first_ref[...], prev_ref[slot])
        r1, r2 = pltpu.roll(u, 1, 0), pltpu.roll(u, 2, 0)
        u1 = jnp.concatenate([jnp.where(row == 0, prev[7:8, :], r1[:8])] + [r1[8:]], axis=0)
        u2 = jnp.concatenate([jnp.where(row == 0, prev[6:7, :], jnp.where(row == 1, prev[7:8, :], r2[:8]))]
                             + [r2[8:]], axis=0)
        cw = cw_ref[...]
        tail = u[tm - 8:, :]
        prev_ref[slot] = tail
        tail_ref[...] = tail
        return cw[2:3, :] * u + cw[1:2, :] * u1 + cw[0:1, :] * u2

    def up_matmuls():
        h = h_ref[...]
        return _dot(h, wa_ref[...]), _dot(h, wg_ref[...])

    def activation(ua, ug):
        a = conv_half(ua, cwa_ref, pa_ref, c, ta_ref)
        g = conv_half(ug, cwg_ref, pg_ref, n_chunks + c, tg_ref)
        act_ref[c % 2] = (_gelu_tanh(a) * g).astype(bf16)

    def down():
        return _dot(act_ref[(c + 1) % 2], wd_ref[...])

    @pl.when(c == 0)
    def _():
        acc_ref[...] = jnp.zeros_like(acc_ref)
        activation(*up_matmuls())

    @pl.when((c > 0) & (c < n_chunks))
    def _():
        acc_ref[...] += down()
        activation(*up_matmuls())

    @pl.when(c == n_chunks)
    def _():
        y_ref[...] = _rmsnorm(x_ref[...] + acc_ref[...] + down(), gf_ref[...])


def _ffn_prompt(h, x1, wup, cw, first_a, first_g, wd, gf, tm, rows_per_seq):
    rows = h.shape[0]
    fc = FF_CHUNK
    nch = D_FF // fc
    nt = rows // tm
    up_c = lambda c: jnp.minimum(c, nch - 1)
    return pl.pallas_call(
        functools.partial(_ffn_prompt_kernel, tiles_per_seq=rows_per_seq // tm, n_chunks=nch),
        out_shape=(jax.ShapeDtypeStruct((rows, D_MODEL), f32),
                   jax.ShapeDtypeStruct((nt, 8, D_FF), f32), jax.ShapeDtypeStruct((nt, 8, D_FF), f32)),
        grid=(nt, nch + 1),
        in_specs=[pl.BlockSpec((tm, D_MODEL), lambda i, c: (i, 0)),
                  pl.BlockSpec((tm, D_MODEL), lambda i, c: (i, 0)),
                  pl.BlockSpec((D_MODEL, fc), lambda i, c: (0, up_c(c))),
                  pl.BlockSpec((D_MODEL, fc), lambda i, c: (0, nch + up_c(c))),
                  pl.BlockSpec((CONV_W, fc), lambda i, c: (0, up_c(c))),
                  pl.BlockSpec((CONV_W, fc), lambda i, c: (0, nch + up_c(c))),
                  pl.BlockSpec((8, fc), lambda i, c: (0, up_c(c))),
                  pl.BlockSpec((8, fc), lambda i, c: (0, up_c(c))),
                  pl.BlockSpec((fc, D_MODEL), lambda i, c: (jnp.maximum(c - 1, 0), 0)),
                  pl.BlockSpec((1, D_MODEL), lambda i, c: (0, 0))],
        out_specs=(pl.BlockSpec((tm, D_MODEL), lambda i, c: (i, 0)),
                   pl.BlockSpec((None, 8, fc), lambda i, c: (i, 0, up_c(c))),
                   pl.BlockSpec((None, 8, fc), lambda i, c: (i, 0, up_c(c)))),
        scratch_shapes=[pltpu.VMEM((tm, D_MODEL), f32), pltpu.VMEM((2 * nch, 8, fc), f32),
                        pltpu.VMEM((2, tm, fc), bf16)],
        compiler_params=_cparams(("arbitrary", "arbitrary")),
        name="conv_ffn_prompt",
    )(h, x1, wup, wup, cw, cw, first_a, first_g, wd, gf)


def _ffn_small_kernel(h_ref, x_ref, wa_ref, wg_ref, cwa_ref, cwg_ref, sta_ref, stg_ref, wd_ref, gf_ref,
                      y_ref, ua_ref, ug_ref, acc_ref, *, n_sample, n_tok):
    c = pl.program_id(1)
    rows = h_ref.shape[0]
    h = h_ref[...]
    row = lax.broadcasted_iota(jnp.int32, (rows, FF_CHUNK), 0)
    t = jnp.where(row < n_sample, row % n_tok, row - n_sample)

    def conv_half(w_ref, cw_ref, st_ref, u_ref):
        u = _dot(h, w_ref[...])
        u_ref[...] = u
        st = st_ref[...]
        u1 = jnp.where(t == 0, pltpu.roll(st, rows - 1, 0), pltpu.roll(u, 1, 0))
        u2 = jnp.where(t < 2, st, pltpu.roll(u, 2, 0))
        cw = cw_ref[...]
        return cw[2:3, :] * u + cw[1:2, :] * u1 + cw[0:1, :] * u2

    a = conv_half(wa_ref, cwa_ref, sta_ref, ua_ref)
    g = conv_half(wg_ref, cwg_ref, stg_ref, ug_ref)
    _ffn_finish(c, _gelu_tanh(a) * g, wd_ref, x_ref, gf_ref, y_ref, acc_ref)


def _ffn_small(h, x1, wup, cw, st, wd, gf, n_sample, n_tok):
    rows = h.shape[0]
    fc = FF_CHUNK
    nch = D_FF // fc
    return pl.pallas_call(
        functools.partial(_ffn_small_kernel, n_sample=n_sample, n_tok=n_tok),
        out_shape=(jax.ShapeDtypeStruct((rows, D_MODEL), f32),
                   jax.ShapeDtypeStruct((rows, D_FF), f32), jax.ShapeDtypeStruct((rows, D_FF), f32)),
        grid=(1, nch),
        in_specs=[pl.BlockSpec((rows, D_MODEL), lambda i, c: (0, 0)),
                  pl.BlockSpec((rows, D_MODEL), lambda i, c: (0, 0)),
                  pl.BlockSpec((D_MODEL, fc), lambda i, c: (0, c)),
                  pl.BlockSpec((D_MODEL, fc), lambda i, c: (0, nch + c)),
                  pl.BlockSpec((CONV_W, fc), lambda i, c: (0, c)),
                  pl.BlockSpec((CONV_W, fc), lambda i, c: (0, nch + c)),
                  pl.BlockSpec((rows, fc), lambda i, c: (0, c)),
                  pl.BlockSpec((rows, fc), lambda i, c: (0, nch + c)),
                  pl.BlockSpec((fc, D_MODEL), lambda i, c: (c, 0)),
                  pl.BlockSpec((1, D_MODEL), lambda i, c: (0, 0))],
        out_specs=(pl.BlockSpec((rows, D_MODEL), lambda i, c: (0, 0)),
                   pl.BlockSpec((rows, fc), lambda i, c: (0, c)),
                   pl.BlockSpec((rows, fc), lambda i, c: (0, c))),
        scratch_shapes=[pltpu.VMEM((rows, D_MODEL), f32)],
        compiler_params=_cparams(("arbitrary", "arbitrary")),
        name="conv_ffn_small",
    )(h, x1, wup, wup, cw, cw, st, st, wd, gf)


def _rope_tables(pos):
    half = RET_DK // 2
    inv = ROPE_BASE ** (-jnp.arange(half, dtype=f32) / half)
    ang = pos.astype(f32)[:, None] * inv[None, :]
    c, s = jnp.cos(ang), jnp.sin(ang)
    return jnp.concatenate([c, c], axis=-1), jnp.concatenate([-s, s], axis=-1)


def _layer(xp, xs, ck, cv, page_table, s_ret, s_conv, meta, norm1_g, w_in, sb_bias, w_ret_proj, w_sb_proj, w_o,
           norm2_g, w_up, conv_w, w_down, norm_f_g):
    batch, seq, d = xp.shape
    nb, n_tok, _ = xs.shape
    n_pool, page = ck.shape[0], ck.shape[1]
    n_pages = page_table.shape[1]
    past = n_pages * page
    n_sample = nb * n_tok
    n_small = n_sample + N_META

    w_in, w_ret_proj, w_sb_proj, w_o, w_up, w_down = (
        w.astype(bf16) for w in (w_in, w_ret_proj, w_sb_proj, w_o, w_up, w_down))
    g1, g2, gf = norm1_g[None, :], norm2_g[None, :], norm_f_g[None, :]
    x_big = xp.reshape(batch * seq, d)
    x_small = jnp.concatenate([xs.reshape(n_sample, d), meta.astype(f32)], axis=0)

    tm_big = 1024
    proj_s = _norm_mm(x_small, g1, w_in, n_small, 1024)
    proj_p = _norm_mm(x_big, g1, w_in, tm_big, 1024)

    lg = jnp.log1p(-jnp.exp2(-5.0 - jnp.arange(RET_HEADS, dtype=f32)))
    lg = jnp.broadcast_to(lg[:, None, None], (RET_HEADS, 1, RET_DV))
    cos_m, sin_m = _rope_tables(jnp.arange(N_META))
    ret_m, s_meta = _ret_small(proj_s, n_sample, N_META, N_META, cos_m, sin_m, lg,
                               jnp.zeros((1, RET_HEADS, RET_DK, RET_DV), f32))
    cos_p, sin_p = _rope_tables(N_META + jnp.arange(seq))
    ret_p, sret_p = _ret_prompt(proj_p, cos_p, sin_p, lg, s_meta[0], batch, seq)
    cos_s, sin_s = _rope_tables(past + (jnp.arange(SMALL_ROWS) % n_tok))
    ret_s, sret_s = _ret_small(proj_s, 0, n_sample, n_tok, cos_s, sin_s, lg, s_ret)

    tri = (jnp.arange(SB_BK)[:, None] > jnp.arange(SB_BK)[None, :]).astype(bf16)
    pad_rows = lambda a, n: jnp.pad(a, [(0, 0)] * (a.ndim - 2) + [(0, n - a.shape[-2]), (0, 0)])
    km = pad_rows(proj_s[n_sample:, C_SK:C_SK + SB_W], 128).astype(bf16)
    vm = pad_rows(proj_s[n_sample:, C_SV:C_SV + SB_W], 128).astype(bf16)
    sb_m = _sb_meta(proj_s, n_sample, km, vm, sb_bias, tri)
    ps3 = proj_s[:n_sample].reshape(nb, n_tok, IN_WIDTH)
    cache_t = lambda c: jnp.transpose(c, (0, 2, 3, 1)).reshape(n_pool, SB_W, page)
    bias_col = jnp.repeat(sb_bias * LOG2E, n_tok)[:, None]
    sb_p, sb_s = _stick_breaking(proj_p, km, vm, sb_bias, tri, batch, seq,
                                 page_table, ps3, cache_t(ck), cache_t(cv), bias_col)

    ret_small = jnp.concatenate([ret_s, ret_m], axis=0)
    sb_small = jnp.concatenate([sb_s.reshape(n_sample, SB_W).astype(bf16), sb_m], axis=0)
    m_s = _gate_mix(ret_small, sb_small, proj_s, w_ret_proj, w_sb_proj, n_small, 512)
    m_p = _gate_mix(ret_p, sb_p, proj_p, w_ret_proj, w_sb_proj, tm_big, 512)
    x1_s, h2_s = _out_proj(x_small, m_s, w_o, g2, n_small)
    x1_p, h2_p = _out_proj(x_big, m_p, w_o, g2, 512)

    st = jnp.concatenate([jnp.pad(s_conv, ((0, 0), (0, n_tok - (CONV_W - 1)), (0, 0))).reshape(n_sample, 2 * D_FF),
                          jnp.zeros((N_META, 2 * D_FF), f32)], axis=0)
    y_s, ua_s, ug_s = _ffn_small(h2_s, x1_s, w_up, conv_w, st, w_down, gf, n_sample, n_tok)
    y_p, ta_p, tg_p = _ffn_prompt(h2_p, x1_p, w_up, conv_w, ua_s[n_small - 8:], ug_s[n_small - 8:], w_down, gf, 512, seq)

    y_prompt = y_p.reshape(batch, seq, d)
    y_sample = y_s[:n_sample].reshape(nb, n_tok, d)

    kt, vt = _kv_out(proj_p, proj_s, n_sample, batch, seq)
    token_major = lambda a: jnp.transpose(a.reshape(batch, SB_HEADS, SB_DH, N_META + seq), (0, 3, 1, 2))
    k_p, v_p = token_major(kt), token_major(vt)
    k_s = ps3[:, :, C_SK:C_SK + SB_W].reshape(nb, n_tok, SB_HEADS, SB_DH)
    v_s = ps3[:, :, C_SV:C_SV + SB_W].reshape(nb, n_tok, SB_HEADS, SB_DH)
    tiles_per_seq = seq // 512
    last = lambda tl: tl.reshape(batch, tiles_per_seq, 8, D_FF)[:, -1, 8 - (CONV_W - 1):, :]
    conv_p = jnp.concatenate([last(ta_p), last(tg_p)], axis=-1)
    u_s = jnp.concatenate([ua_s[:n_sample], ug_s[:n_sample]], axis=-1).reshape(nb, n_tok, 2 * D_FF)
    conv_s = u_s[:, n_tok - (CONV_W - 1):, :]
    return y_prompt, y_sample, k_p, v_p, k_s, v_s, sret_p, sret_s, conv_p, conv_s


def kernel(x_prompt, x_sample, cache_k, cache_v, page_table, state_ret, state_conv, meta_tokens, norm1_g, w_in,
           sb_bias, w_ret_proj, w_sb_proj, w_o, norm2_g, w_up, conv_w, w_down, norm_f_g):
    assert cache_k.shape[0] == 1, "one layer"
    outs = _layer(x_prompt, x_sample, cache_k[0], cache_v[0], page_table, state_ret[0], state_conv[0], meta_tokens,
                  norm1_g[0], w_in[0], sb_bias[0], w_ret_proj[0], w_sb_proj[0], w_o[0], norm2_g[0], w_up[0],
                  conv_w[0], w_down[0], norm_f_g)
    y_prompt, y_sample = outs[0], outs[1]
    return (y_prompt, y_sample) + tuple(o[None] for o in outs[2:])
```

```python
import functools

import jax
import jax.numpy as jnp
from jax import lax
from jax.experimental import pallas as pl
from jax.experimental.pallas import tpu as pltpu

f32 = jnp.float32
bf16 = jnp.bfloat16

D_MODEL = 2048
N_META = 16
RET_HEADS = 8
RET_DK = 128
RET_DV = 256
SB_HEADS = 16
SB_DH = 64
D_FF = 5632
CONV_W = 3
ROPE_BASE = 10000.0
EPS = 1e-6
GN_EPS = 1e-5

RET_QK = RET_HEADS * RET_DK
RET_V = RET_HEADS * RET_DV
SB_W = SB_HEADS * SB_DH
C_RQ, C_RK, C_RV, C_RG = 0, RET_QK, 2 * RET_QK, 2 * RET_QK + RET_V
C_SQ = C_RG + RET_V
C_SK, C_SV = C_SQ + SB_W, C_SQ + 2 * SB_W
C_GR = C_SV + SB_W
C_GS = C_GR + D_MODEL
IN_WIDTH = C_GS + D_MODEL

RET_CHUNK = 128
RET_HEAD_GROUP = 8
SB_BQ = 512
SB_BK = 256
LOG2E = 1.4426950408889634
SB_LOGIT_CLAMP = 100.0
SB_HEAD_GROUP = 4
SMALL_ROWS = 16
FF_CHUNK = 512
VMEM_LIMIT = 56 * 1024 * 1024


def _cparams(sem):
    return pltpu.CompilerParams(dimension_semantics=sem, vmem_limit_bytes=VMEM_LIMIT)


def _dot(a, b):
    return jnp.dot(a, b, preferred_element_type=f32)


def _dot_nt(a, b):
    return lax.dot_general(a, b, (((1,), (1,)), ((), ())), preferred_element_type=f32)


def _sigmoid(x):
    return 1.0 / (1.0 + jnp.exp(-x))


def _rmsnorm(x, g):
    return x * lax.rsqrt(jnp.mean(x * x, axis=-1, keepdims=True) + EPS) * g


def _norm_mm_kernel(x_ref, g_ref, w_ref, o_ref, h_ref):
    @pl.when(pl.program_id(1) == 0)
    def _():
        h_ref[...] = _rmsnorm(x_ref[...], g_ref[...]).astype(bf16)

    o_ref[...] = _dot(h_ref[...], w_ref[...])


def _norm_mm(x, g, w, tm, tn):
    m, d = x.shape
    n = w.shape[1]
    return pl.pallas_call(
        _norm_mm_kernel,
        out_shape=jax.ShapeDtypeStruct((m, n), f32),
        grid=(m // tm, n // tn),
        in_specs=[pl.BlockSpec((tm, d), lambda i, j: (i, 0)),
                  pl.BlockSpec((1, d), lambda i, j: (0, 0)),
                  pl.BlockSpec((d, tn), lambda i, j: (0, j))],
        out_specs=pl.BlockSpec((tm, tn), lambda i, j: (i, j)),
        scratch_shapes=[pltpu.VMEM((tm, d), bf16)],
        compiler_params=_cparams(("parallel", "arbitrary")),
        name="norm_in_proj",
    )(x, g, w)


def _rope(x, cos2, sin2):
    return x * cos2 + pltpu.roll(x, RET_DK // 2, 1) * sin2


def _gated_groupnorm(o, gate):
    mu = jnp.mean(o, axis=-1, keepdims=True)
    oc = o - mu
    var = jnp.mean(oc * oc, axis=-1, keepdims=True)
    return (gate * _sigmoid(gate)) * (oc * lax.rsqrt(var + GN_EPS))


def _ret_prompt_kernel(q_ref, k_ref, v_ref, g_ref, cos_ref, sin_ref, lg_ref, s0_ref, o_ref, sfin_ref, s_ref, dec_ref):
    c = pl.program_id(2)
    n = RET_CHUNK

    @pl.when(c == 0)
    def _():
        s_ref[...] = s0_ref[...]
        row = lax.broadcasted_iota(jnp.int32, (n, RET_DK), 0).astype(f32)
        diff = row - lax.broadcasted_iota(jnp.int32, (n, n), 1).astype(f32)
        for h in range(RET_HEAD_GROUP):
            lg = lg_ref[h][:, :RET_DK]
            dec_ref[h, 0] = jnp.where(diff >= 0, jnp.exp(diff * lg), 0.0)
            dec_ref[h, 1] = jnp.exp((row + 1.0) * lg)
            dec_ref[h, 2] = jnp.exp((n - 1.0 - row) * lg)

    cos2, sin2 = cos_ref[...], sin_ref[...]
    for h in range(RET_HEAD_GROUP):
        dk = slice(h * RET_DK, (h + 1) * RET_DK)
        dv = slice(h * RET_DV, (h + 1) * RET_DV)
        q = _rope(q_ref[:, dk], cos2, sin2)
        k = _rope(k_ref[:, dk], cos2, sin2) * (RET_DK ** -0.5)
        att = _dot_nt(q.astype(bf16), k.astype(bf16)) * dec_ref[h, 0]
        vb = v_ref[:, dv].astype(bf16)
        s = s_ref[h]
        o = _dot(att.astype(bf16), vb) + _dot((q * dec_ref[h, 1]).astype(bf16), s.astype(bf16))
        kd = k * dec_ref[h, 2]
        s_ref[h] = jnp.exp(n * lg_ref[h]) * s + _dot(kd.T.astype(bf16), vb)
        o_ref[:, dv] = _gated_groupnorm(o, g_ref[:, dv]).astype(bf16)

    @pl.when(c == pl.num_programs(2) - 1)
    def _():
        sfin_ref[...] = s_ref[...]


def _ret_prompt(proj, cos2, sin2, lg, s0, batch, seq):
    nc = seq // RET_CHUNK
    n = RET_CHUNK
    assert n == RET_DK, "the decay scratch stores (n, n) and (n, RET_DK) factors in one array"
    hg = RET_HEAD_GROUP
    wk, wv = hg * RET_DK, hg * RET_DV
    row = lambda b, h, c: b * nc + c
    return pl.pallas_call(
        _ret_prompt_kernel,
        out_shape=(jax.ShapeDtypeStruct((batch * seq, RET_V), bf16),
                   jax.ShapeDtypeStruct((batch, RET_HEADS, RET_DK, RET_DV), f32)),
        grid=(batch, RET_HEADS // hg, nc),
        in_specs=[pl.BlockSpec((n, wk), lambda b, h, c: (row(b, h, c), C_RQ // wk + h)),
                  pl.BlockSpec((n, wk), lambda b, h, c: (row(b, h, c), C_RK // wk + h)),
                  pl.BlockSpec((n, wv), lambda b, h, c: (row(b, h, c), C_RV // wv + h)),
                  pl.BlockSpec((n, wv), lambda b, h, c: (row(b, h, c), C_RG // wv + h)),
                  pl.BlockSpec((n, RET_DK), lambda b, h, c: (c, 0)),
                  pl.BlockSpec((n, RET_DK), lambda b, h, c: (c, 0)),
                  pl.BlockSpec((hg, 1, RET_DV), lambda b, h, c: (h, 0, 0)),
                  pl.BlockSpec((hg, RET_DK, RET_DV), lambda b, h, c: (h, 0, 0))],
        out_specs=(pl.BlockSpec((n, wv), lambda b, h, c: (row(b, h, c), h)),
                   pl.BlockSpec((None, hg, RET_DK, RET_DV), lambda b, h, c: (b, h, 0, 0))),
        scratch_shapes=[pltpu.VMEM((hg, RET_DK, RET_DV), f32), pltpu.VMEM((hg, 3, n, RET_DK), f32)],
        compiler_params=_cparams(("arbitrary", "arbitrary", "arbitrary")),
        name="retention_prompt",
    )(proj, proj, proj, proj, cos2, sin2, lg, s0)


def _ret_small_kernel(q_ref, k_ref, v_ref, g_ref, cos_ref, sin_ref, lg_ref, s0_ref, o_ref, s_ref, *, n_tok):
    r = SMALL_ROWS
    n_seq = r // n_tok
    ri = lax.broadcasted_iota(jnp.int32, (r, RET_DK), 0)
    ti = (ri % n_tok).astype(f32)
    rj = lax.broadcasted_iota(jnp.int32, (r, 128), 1)
    rr = lax.broadcasted_iota(jnp.int32, (r, 128), 0)
    tdiff = ((rr % n_tok) - (rj % n_tok)).astype(f32)
    visible = ((rr // n_tok) == (rj // n_tok)) & (tdiff >= 0)
    rk = lax.broadcasted_iota(jnp.int32, (128, RET_DK), 0)
    tk = (rk % n_tok).astype(f32)
    rv = lax.broadcasted_iota(jnp.int32, (r, RET_DV), 0)
    cos2, sin2 = cos_ref[...], sin_ref[...]
    pad = 128 - r
    for h in range(RET_HEADS):
        lgv = lg_ref[h]
        lg = lgv[:, :RET_DK]
        q = _rope(q_ref[:, h * RET_DK:(h + 1) * RET_DK], cos2, sin2)
        k = _rope(k_ref[:, h * RET_DK:(h + 1) * RET_DK], cos2, sin2) * (RET_DK ** -0.5)
        kpad = jnp.concatenate([k, jnp.zeros((pad, RET_DK), f32)], axis=0)
        vpad = jnp.concatenate([v_ref[:, h * RET_DV:(h + 1) * RET_DV], jnp.zeros((pad, RET_DV), f32)],
                               axis=0).astype(bf16)
        decay = jnp.where(visible, jnp.exp(tdiff * lg), 0.0)
        att = _dot_nt(q.astype(bf16), kpad.astype(bf16)) * decay
        o = _dot(att.astype(bf16), vpad)
        qd = (q * jnp.exp((ti + 1.0) * lg)).astype(bf16)
        kd = kpad * jnp.exp((n_tok - 1.0 - tk) * lg)
        for s_i in range(n_seq):
            s = s0_ref[s_i, h]
            mine_k = (rk // n_tok) == s_i
            mine_v = (rv // n_tok) == s_i
            o = o + jnp.where(mine_v, _dot(qd, s.astype(bf16)), 0.0)
            s_ref[s_i, h] = jnp.exp(n_tok * lgv) * s + _dot(jnp.where(mine_k, kd, 0.0).T.astype(bf16), vpad)
        o_ref[:, h * RET_DV:(h + 1) * RET_DV] = _gated_groupnorm(
            o, g_ref[:, h * RET_DV:(h + 1) * RET_DV]).astype(bf16)


def _ret_small(proj, row0, n_rows, n_tok, cos2, sin2, lg, s0):
    r = SMALL_ROWS
    n_seq = r // n_tok
    steps = n_rows // r
    b0 = row0 // r
    return pl.pallas_call(
        functools.partial(_ret_small_kernel, n_tok=n_tok),
        out_shape=(jax.ShapeDtypeStruct((n_rows, RET_V), bf16),
                   jax.ShapeDtypeStruct(s0.shape, f32)),
        grid=(steps,),
        in_specs=[pl.BlockSpec((r, RET_QK), lambda i: (b0 + i, C_RQ // RET_QK)),
                  pl.BlockSpec((r, RET_QK), lambda i: (b0 + i, C_RK // RET_QK)),
                  pl.BlockSpec((r, RET_V), lambda i: (b0 + i, C_RV // RET_V)),
                  pl.BlockSpec((r, RET_V), lambda i: (b0 + i, C_RG // RET_V)),
                  pl.BlockSpec((r, RET_DK), lambda i: (0, 0)),
                  pl.BlockSpec((r, RET_DK), lambda i: (0, 0)),
                  pl.BlockSpec((RET_HEADS, 1, RET_DV), lambda i: (0, 0, 0)),
                  pl.BlockSpec((n_seq, RET_HEADS, RET_DK, RET_DV), lambda i: (i, 0, 0, 0))],
        out_specs=(pl.BlockSpec((r, RET_V), lambda i: (i, 0)),
                   pl.BlockSpec((n_seq, RET_HEADS, RET_DK, RET_DV), lambda i: (i, 0, 0, 0))),
        compiler_params=_cparams(("parallel",)),
        name="retention_small",
    )(proj, proj, proj, proj, cos2, sin2, lg, s0)


def _sb_log2_keep(z2, mask):
    zc = jnp.minimum(z2, SB_LOGIT_CLAMP)
    lk = jnp.log(1.0 + jnp.exp2(zc)) * -LOG2E
    return zc, (lk if mask is None else jnp.where(mask, lk, 0.0))


def _sb_terms(z2, tri, mask):
    zc, lk = _sb_log2_keep(z2, mask)
    between = _dot(lk.astype(bf16), tri)
    w = jnp.exp2(lk + between + zc)
    if mask is not None:
        w = jnp.where(mask, w, 0.0)
    return w.astype(bf16), between[:, :1] + lk[:, :1]


def _sb_kernel(pt_ref, bias_ref, q_ref, k_ref, v_ref, km_ref, vm_ref, tri_ref, qs_ref, kn_ref, vn_ref, *rest,
               n_tok, n_grp):
    k_refs, v_refs = rest[:n_grp], rest[n_grp:2 * n_grp]
    bcol_ref, o_ref, os_ref, acc_ref, r_ref, mc_ref, mt_ref, qg_ref, accs_ref, rs_ref, new_ref = rest[2 * n_grp:]
    p, i = pl.program_id(1), pl.program_id(2)
    bq, bk = SB_BQ, SB_BK
    tri = tri_ref[...]

    page = k_refs[0].shape[1]
    rows = n_tok * SB_HEADS
    grp_rows = n_tok * SB_HEAD_GROUP
    grp_cols = SB_DH * SB_HEAD_GROUP
    n_row_grp = SB_HEADS // SB_HEAD_GROUP
    own = (lax.broadcasted_iota(jnp.int32, (grp_rows, grp_cols), 1) // SB_DH
           == lax.broadcasted_iota(jnp.int32, (grp_rows, grp_cols), 0) // n_tok)
    tri_p = tri[:page, :page]
    row_grp = lambda a, g: a[g * grp_rows:(g + 1) * grp_rows]
    col_grp = lambda g: slice(g * grp_cols, (g + 1) * grp_cols)
    by_row_grp = lambda f: jnp.concatenate([f(g) for g in range(n_row_grp)], axis=0)

    def sample_blocks(logits, weigh, mask, init):
        zc, lk = zip(*[_sb_log2_keep(z2 + bcol_ref[...], mask) for z2 in logits])
        between = _dot(jnp.concatenate([l.astype(bf16) for l in lk], axis=0), tri_p)
        r = jnp.zeros((rows, 1), f32) if init else rs_ref[...]
        acc = None if init else accs_ref[...]
        for g in range(len(logits)):
            btw = between[g * rows:(g + 1) * rows]
            w = jnp.exp2(lk[g] + btw + r + zc[g])
            if mask is not None:
                w = jnp.where(mask, w, 0.0)
            r = r + btw[:, :1] + lk[g][:, :1]
            contrib = weigh[g](w.astype(bf16))
            acc = contrib if acc is None else acc + contrib
        rs_ref[...] = r
        accs_ref[...] = acc

    @pl.when(i == 0)
    def _():
        qs = qs_ref[...] * (SB_DH ** -0.5 * LOG2E)
        for h in range(SB_HEADS):
            g = h // SB_HEAD_GROUP
            qg_ref[h * n_tok:(h + 1) * n_tok, :] = jnp.where(
                own[(h % SB_HEAD_GROUP) * n_tok:(h % SB_HEAD_GROUP + 1) * n_tok], qs[:, col_grp(g)], 0.0)
        new_ref[...] = jnp.zeros_like(new_ref)
        new_ref[0, :n_tok, :] = kn_ref[...]
        new_ref[1, :n_tok, :] = vn_ref[...]
        qg = qg_ref[...].astype(bf16)
        kk = lax.broadcasted_iota(jnp.int32, (rows, page), 1)
        tq = lax.broadcasted_iota(jnp.int32, (rows, page), 0) % n_tok
        z_new = by_row_grp(lambda g: _dot_nt(row_grp(qg, g), new_ref[0, :, col_grp(g)].astype(bf16)))
        wv_new = lambda w: by_row_grp(lambda g: _dot(row_grp(w, g), new_ref[1, :, col_grp(g)].astype(bf16)))
        sample_blocks([z_new], [wv_new], kk < tq, True)

    qg = qg_ref[...].astype(bf16)
    sample_blocks(
        [by_row_grp(lambda g, r_=r_: _dot(row_grp(qg, g), r_[col_grp(g), :].astype(bf16))) for r_ in k_refs],
        [lambda w, r_=r_: by_row_grp(lambda g: _dot_nt(row_grp(w, g), r_[col_grp(g), :].astype(bf16)))
         for r_ in v_refs], None, False)

    q = q_ref[...] * (SB_DH ** -0.5 * LOG2E)
    lane = lax.broadcasted_iota(jnp.int32, (bq, 2 * SB_DH), 1)
    first = lane < SB_DH
    qm = (jnp.where(first, q, 0.0).astype(bf16), jnp.where(first, 0.0, q).astype(bf16))
    bias = (bias_ref[2 * p] * LOG2E, bias_ref[2 * p + 1] * LOG2E)

    def terms(kb, vb, mask, tri_b):
        out = []
        for hh in range(2):
            w, total = _sb_terms(_dot_nt(qm[hh], kb) + bias[hh], tri_b, mask)
            out.append((_dot(w, vb), total))
        return out

    def accumulate(res):
        for hh, (contrib, total) in enumerate(res):
            r = r_ref[hh]
            acc_ref[hh] += jnp.exp2(r) * contrib
            r_ref[hh] = r + total

    def kv(j):
        s0 = pl.multiple_of(j * bk, bk)
        return k_ref[pl.ds(s0, bk), :].astype(bf16), v_ref[pl.ds(s0, bk), :].astype(bf16)

    nkb = bq // bk
    kk = lax.broadcasted_iota(jnp.int32, (bq, bk), 1)
    qq = lax.broadcasted_iota(jnp.int32, (bq, bk), 0)
    for d in reversed(range(nkb)):
        res = terms(*kv(i * nkb + d), kk + d * bk < qq, tri)
        if d == nkb - 1:
            for hh, (contrib, total) in enumerate(res):
                acc_ref[hh] = contrib
                r_ref[hh] = total
        else:
            accumulate(res)

    nm = km_ref.shape[0]
    km_valid = lax.broadcasted_iota(jnp.int32, (bq, nm), 1) < N_META
    for hh, (contrib, total) in enumerate(terms(km_ref[...], vm_ref[...], km_valid, tri[:nm, :nm])):
        mc_ref[hh] = contrib
        mt_ref[hh] = total

    def pair(t, carry):
        newer = terms(*kv(i * nkb - 1 - 2 * t), None, tri)
        older = terms(*kv(i * nkb - 2 - 2 * t), None, tri)
        accumulate(newer)
        accumulate(older)
        return carry

    lax.fori_loop(0, i * (nkb // 2), pair, 0)

    accumulate([(mc_ref[hh], mt_ref[hh]) for hh in range(2)])
    o_ref[...] = jnp.where(first, acc_ref[0], acc_ref[1]).astype(bf16)

    @pl.when(i == pl.num_programs(2) - 1)
    def _():
        for g in range(n_row_grp):
            sel = jnp.where(own, row_grp(accs_ref[...], g), 0.0)
            os_ref[:, col_grp(g)] = sum(sel[hh * n_tok:(hh + 1) * n_tok] for hh in range(SB_HEAD_GROUP))


def _stick_breaking(proj, km, vm, bias, tri, batch, seq, page_table, proj3, ck, cv, bias_col):
    nq = seq // SB_BQ
    hp = 2 * SB_DH
    n_pairs = SB_HEADS // 2
    nb, n_tok, _ = proj3.shape
    n_pages = page_table.shape[1]
    page = ck.shape[2]
    rows = n_tok * SB_HEADS
    assert nb == batch * n_pairs and n_pages % nq == 0
    n_grp = n_pages // nq

    def page_spec(g):
        return pl.BlockSpec((None, SB_W, page),
                            lambda b, p, i, pt, bs: (pt[b * n_pairs + p, n_pages - 1 - (i * n_grp + g)], 0, 0))

    new_tok = lambda col: pl.BlockSpec((None, n_tok, SB_W), lambda b, p, i, pt, bs: (b * n_pairs + p, 0, col // SB_W))
    return pl.pallas_call(
        functools.partial(_sb_kernel, n_tok=n_tok, n_grp=n_grp),
        out_shape=(jax.ShapeDtypeStruct((batch * seq, SB_W), bf16), jax.ShapeDtypeStruct((nb, n_tok, SB_W), f32)),
        grid_spec=pltpu.PrefetchScalarGridSpec(
            num_scalar_prefetch=2,
            grid=(batch, n_pairs, nq),
            in_specs=([pl.BlockSpec((SB_BQ, hp), lambda b, p, i, pt, bs: (b * nq + i, C_SQ // hp + p)),
                       pl.BlockSpec((seq, hp), lambda b, p, i, pt, bs: (b, C_SK // hp + p)),
                       pl.BlockSpec((seq, hp), lambda b, p, i, pt, bs: (b, C_SV // hp + p)),
                       pl.BlockSpec((km.shape[0], hp), lambda b, p, i, pt, bs: (0, p)),
                       pl.BlockSpec((km.shape[0], hp), lambda b, p, i, pt, bs: (0, p)),
                       pl.BlockSpec((SB_BK, SB_BK), lambda b, p, i, pt, bs: (0, 0)),
                       new_tok(C_SQ), new_tok(C_SK), new_tok(C_SV)]
                      + [page_spec(g) for g in range(n_grp)] + [page_spec(g) for g in range(n_grp)]
                      + [pl.BlockSpec((rows, 1), lambda b, p, i, pt, bs: (0, 0))]),
            out_specs=(pl.BlockSpec((SB_BQ, hp), lambda b, p, i, pt, bs: (b * nq + i, p)),
                       new_tok(0)),
            scratch_shapes=[pltpu.VMEM((2, SB_BQ, hp), f32), pltpu.VMEM((2, SB_BQ, 1), f32),
                            pltpu.VMEM((2, SB_BQ, hp), f32), pltpu.VMEM((2, SB_BQ, 1), f32),
                            pltpu.VMEM((rows, SB_DH * SB_HEAD_GROUP), f32),
                            pltpu.VMEM((rows, SB_DH * SB_HEAD_GROUP), f32),
                            pltpu.VMEM((rows, 1), f32), pltpu.VMEM((2, page, SB_W), f32)]),
        compiler_params=_cparams(("arbitrary", "arbitrary", "arbitrary")),
        name="stick_breaking",
    )(page_table, bias, proj, proj, proj, km, vm, tri, proj3, proj3, proj3,
      *([ck] * n_grp), *([cv] * n_grp), bias_col)


def _sb_meta_kernel(bias_ref, q_ref, km_ref, vm_ref, tri_ref, o_ref):
    p = pl.program_id(0)
    nq, nm = q_ref.shape[0], km_ref.shape[0]
    q = q_ref[...] * (SB_DH ** -0.5 * LOG2E)
    first = lax.broadcasted_iota(jnp.int32, (nq, 2 * SB_DH), 1) < SB_DH
    mask = lax.broadcasted_iota(jnp.int32, (nq, nm), 1) < lax.broadcasted_iota(jnp.int32, (nq, nm), 0)
    tri = tri_ref[...][:nm, :nm]
    kb, vb = km_ref[...], vm_ref[...]
    out = []
    for hh, qh in enumerate((jnp.where(first, q, 0.0), jnp.where(first, 0.0, q))):
        z2 = _dot_nt(qh.astype(bf16), kb) + bias_ref[2 * p + hh] * LOG2E
        w, _ = _sb_terms(z2, tri, mask)
        out.append(_dot(w, vb))
    o_ref[...] = jnp.where(first, out[0], out[1]).astype(bf16)


def _sb_meta(proj_small, row0, km, vm, bias, tri):
    hp = 2 * SB_DH
    nm = km.shape[0]
    return pl.pallas_call(
        _sb_meta_kernel,
        out_shape=jax.ShapeDtypeStruct((N_META, SB_W), bf16),
        grid_spec=pltpu.PrefetchScalarGridSpec(
            num_scalar_prefetch=1,
            grid=(SB_HEADS // 2,),
            in_specs=[pl.BlockSpec((N_META, hp), lambda p, s: (row0 // N_META, C_SQ // hp + p)),
                      pl.BlockSpec((nm, hp), lambda p, s: (0, p)),
                      pl.BlockSpec((nm, hp), lambda p, s: (0, p)),
                      pl.BlockSpec((SB_BK, SB_BK), lambda p, s: (0, 0))],
            out_specs=pl.BlockSpec((N_META, hp), lambda p, s: (0, p))),
        compiler_params=_cparams(("parallel",)),
        name="stick_breaking_meta",
    )(bias, proj_small, km, vm, tri)


def _kv_out_kernel(mk_ref, k_ref, mv_ref, v_ref, ok_ref, ov_ref):
    seq = k_ref.shape[0]
    lanes = k_ref.shape[1]
    for m_ref, x_ref, o_ref in ((mk_ref, k_ref, ok_ref), (mv_ref, v_ref, ov_ref)):
        o_ref[:, 0:lanes] = jnp.concatenate([m_ref[...], x_ref[0:lanes - N_META, :]], axis=0).T
        for t in range(1, seq // lanes):
            o_ref[:, t * lanes:(t + 1) * lanes] = x_ref[t * lanes - N_META:(t + 1) * lanes - N_META, :].T
        last = jnp.concatenate([x_ref[seq - N_META:seq, :], jnp.zeros((lanes - N_META, lanes), f32)], axis=0)
        o_ref[:, seq:seq + N_META] = last.T[:, :N_META]


def _kv_out(proj, proj_small, meta_row0, batch, seq):
    lanes = 128
    total = N_META + seq
    meta = lambda col: pl.BlockSpec((N_META, lanes), lambda b, c: (meta_row0 // N_META, col // lanes + c))
    rows = lambda col: pl.BlockSpec((seq, lanes), lambda b, c: (b, col // lanes + c))
    out = pl.BlockSpec((None, lanes, total), lambda b, c: (b, c, 0))
    return pl.pallas_call(
        _kv_out_kernel,
        out_shape=(jax.ShapeDtypeStruct((batch, SB_W, total), f32), jax.ShapeDtypeStruct((batch, SB_W, total), f32)),
        grid=(batch, SB_W // lanes),
        in_specs=[meta(C_SK), rows(C_SK), meta(C_SV), rows(C_SV)],
        out_specs=(out, out),
        compiler_params=_cparams(("parallel", "parallel")),
        name="kv_prompt_out",
    )(proj_small, proj, proj_small, proj)


def _gate_mix_kernel(ret_ref, sb_ref, gr_ref, gs_ref, wr_ref, ws_ref, o_ref):
    m = _sigmoid(gr_ref[...]) * _dot(ret_ref[...], wr_ref[...]) + _sigmoid(gs_ref[...]) * _dot(sb_ref[...], ws_ref[...])
    o_ref[...] = m.astype(bf16)


def _gate_mix(ret, sb, proj, wr, ws, tm, tn):
    m = ret.shape[0]
    return pl.pallas_call(
        _gate_mix_kernel,
        out_shape=jax.ShapeDtypeStruct((m, D_MODEL), bf16),
        grid=(m // tm, D_MODEL // tn),
        in_specs=[pl.BlockSpec((tm, RET_V), lambda i, j: (i, 0)),
                  pl.BlockSpec((tm, SB_W), lambda i, j: (i, 0)),
                  pl.BlockSpec((tm, tn), lambda i, j: (i, C_GR // tn + j)),
                  pl.BlockSpec((tm, tn), lambda i, j: (i, C_GS // tn + j)),
                  pl.BlockSpec((RET_V, tn), lambda i, j: (0, j)),
                  pl.BlockSpec((SB_W, tn), lambda i, j: (0, j))],
        out_specs=pl.BlockSpec((tm, tn), lambda i, j: (i, j)),
        compiler_params=_cparams(("parallel", "arbitrary")),
        name="gate_mix",
    )(ret, sb, proj, proj, wr, ws)


def _out_proj_kernel(x_ref, m_ref, wo_ref, g_ref, x1_ref, h_ref):
    x1 = x_ref[...] + _dot(m_ref[...], wo_ref[...])
    x1_ref[...] = x1
    h_ref[...] = _rmsnorm(x1, g_ref[...]).astype(bf16)


def _out_proj(x, m, wo, g, tm):
    rows = x.shape[0]
    return pl.pallas_call(
        _out_proj_kernel,
        out_shape=(jax.ShapeDtypeStruct((rows, D_MODEL), f32), jax.ShapeDtypeStruct((rows, D_MODEL), bf16)),
        grid=(rows // tm,),
        in_specs=[pl.BlockSpec((tm, D_MODEL), lambda i: (i, 0)),
                  pl.BlockSpec((tm, D_MODEL), lambda i: (i, 0)),
                  pl.BlockSpec((D_MODEL, D_MODEL), lambda i: (0, 0)),
                  pl.BlockSpec((1, D_MODEL), lambda i: (0, 0))],
        out_specs=(pl.BlockSpec((tm, D_MODEL), lambda i: (i, 0)), pl.BlockSpec((tm, D_MODEL), lambda i: (i, 0))),
        compiler_params=_cparams(("parallel",)),
        name="out_proj_norm",
    )(x, m, wo, g)


def _gelu_tanh(x):
    return 0.5 * x * (1.0 + jnp.tanh(0.7978845608028654 * (x + 0.044715 * (x * x * x))))


def _ffn_finish(c, act, wd_ref, x_ref, gf_ref, y_ref, acc_ref):
    contrib = _dot(act.astype(bf16), wd_ref[...])

    @pl.when(c == 0)
    def _():
        acc_ref[...] = contrib

    @pl.when(c > 0)
    def _():
        acc_ref[...] += contrib

    @pl.when(c == pl.num_programs(1) - 1)
    def _():
        y_ref[...] = _rmsnorm(x_ref[...] + acc_ref[...], gf_ref[...])


def _ffn_prompt_kernel(h_ref, x_ref, wa_ref, wg_ref, cwa_ref, cwg_ref, pa_ref, pg_ref, wd_ref, gf_ref, y_ref, ta_ref, tg_ref,
                       prev_ref, act_ref, *, tiles_per_seq, n_chunks):
    i, c = pl.program_id(0), pl.program_id(1)
    tm = h_ref.shape[0]
    seq_start = (i % tiles_per_seq) == 0
    row = lax.broadcasted_iota(jnp.int32, (8, FF_CHUNK), 0)

    def conv_half(u, cw_ref, first_ref, slot, tail_ref):
        prev = jnp.where(seq_start, first_ref[...], prev_ref[slot])
        r1, r2 = pltpu.roll(u, 1, 0), pltpu.roll(u, 2, 0)
        u1 = jnp.concatenate([jnp.where(row == 0, prev[7:8, :], r1[:8])] + [r1[8:]], axis=0)
        u2 = jnp.concatenate([jnp.where(row == 0, prev[6:7, :], jnp.where(row == 1, prev[7:8, :], r2[:8]))]
                             + [r2[8:]], axis=0)
        cw = cw_ref[...]
        tail = u[tm - 8:, :]
        prev_ref[slot] = tail
        tail_ref[...] = tail
        return cw[2:3, :] * u + cw[1:2, :] * u1 + cw[0:1, :] * u2

    def up_matmuls():
        h = h_ref[...]
        return _dot(h, wa_ref[...]), _dot(h, wg_ref[...])

    def activation(ua, ug):
        a = conv_half(ua, cwa_ref, pa_ref, c, ta_ref)
        g = conv_half(ug, cwg_ref, pg_ref, n_chunks + c, tg_ref)
        act_ref[c % 2] = (_gelu_tanh(a) * g).astype(bf16)

    def down():
        return _dot(act_ref[(c + 1) % 2], wd_ref[...])

    @pl.when(c == 0)
    def _():
        y_ref[...] = jnp.zeros_like(y_ref)
        activation(*up_matmuls())

    @pl.when((c > 0) & (c < n_chunks))
    def _():
        y_ref[...] += down()
        activation(*up_matmuls())

    @pl.when(c == n_chunks)
    def _():
        y_ref[...] = _rmsnorm(x_ref[...] + y_ref[...] + down(), gf_ref[...])


def _ffn_prompt(h, x1, wup, cw, first_a, first_g, wd, gf, tm, rows_per_seq):
    rows = h.shape[0]
    fc = FF_CHUNK
    nch = D_FF // fc
    nt = rows // tm
    up_c = lambda c: jnp.minimum(c, nch - 1)
    return pl.pallas_call(
        functools.partial(_ffn_prompt_kernel, tiles_per_seq=rows_per_seq // tm, n_chunks=nch),
        out_shape=(jax.ShapeDtypeStruct((rows, D_MODEL), f32),
                   jax.ShapeDtypeStruct((nt, 8, D_FF), f32), jax.ShapeDtypeStruct((nt, 8, D_FF), f32)),
        grid=(nt, nch + 1),
        in_specs=[pl.BlockSpec((tm, D_MODEL), lambda i, c: (i, 0)),
                  pl.BlockSpec((tm, D_MODEL), lambda i, c: (i, 0), pipeline_mode=pl.Buffered(1)),
                  pl.BlockSpec((D_MODEL, fc), lambda i, c: (0, up_c(c))),
                  pl.BlockSpec((D_MODEL, fc), lambda i, c: (0, nch + up_c(c))),
                  pl.BlockSpec((CONV_W, fc), lambda i, c: (0, up_c(c))),
                  pl.BlockSpec((CONV_W, fc), lambda i, c: (0, nch + up_c(c))),
                  pl.BlockSpec((8, fc), lambda i, c: (0, up_c(c))),
                  pl.BlockSpec((8, fc), lambda i, c: (0, up_c(c))),
                  pl.BlockSpec((fc, D_MODEL), lambda i, c: (jnp.maximum(c - 1, 0), 0)),
                  pl.BlockSpec((1, D_MODEL), lambda i, c: (0, 0))],
        out_specs=(pl.BlockSpec((tm, D_MODEL), lambda i, c: (i, 0)),
                   pl.BlockSpec((None, 8, fc), lambda i, c: (i, 0, up_c(c))),
                   pl.BlockSpec((None, 8, fc), lambda i, c: (i, 0, up_c(c)))),
        scratch_shapes=[pltpu.VMEM((2 * nch, 8, fc), f32), pltpu.VMEM((2, tm, fc), bf16)],
        compiler_params=_cparams(("arbitrary", "arbitrary")),
        name="conv_ffn_prompt",
    )(h, x1, wup, wup, cw, cw, first_a, first_g, wd, gf)


def _ffn_small_kernel(h_ref, x_ref, wa_ref, wg_ref, cwa_ref, cwg_ref, sta_ref, stg_ref, wd_ref, gf_ref,
                      y_ref, ua_ref, ug_ref, acc_ref, *, n_sample, n_tok):
    c = pl.program_id(1)
    rows = h_ref.shape[0]
    h = h_ref[...]
    row = lax.broadcasted_iota(jnp.int32, (rows, FF_CHUNK), 0)
    t = jnp.where(row < n_sample, row % n_tok, row - n_sample)

    def conv_half(w_ref, cw_ref, st_ref, u_ref):
        u = _dot(h, w_ref[...])
        u_ref[...] = u
        st = st_ref[...]
        u1 = jnp.where(t == 0, pltpu.roll(st, rows - 1, 0), pltpu.roll(u, 1, 0))
        u2 = jnp.where(t < 2, st, pltpu.roll(u, 2, 0))
        cw = cw_ref[...]
        return cw[2:3, :] * u + cw[1:2, :] * u1 + cw[0:1, :] * u2

    a = conv_half(wa_ref, cwa_ref, sta_ref, ua_ref)
    g = conv_half(wg_ref, cwg_ref, stg_ref, ug_ref)
    _ffn_finish(c, _gelu_tanh(a) * g, wd_ref, x_ref, gf_ref, y_ref, acc_ref)


def _ffn_small(h, x1, wup, cw, st, wd, gf, n_sample, n_tok):
    rows = h.shape[0]
    fc = FF_CHUNK
    nch = D_FF // fc
    return pl.pallas_call(
        functools.partial(_ffn_small_kernel, n_sample=n_sample, n_tok=n_tok),
        out_shape=(jax.ShapeDtypeStruct((rows, D_MODEL), f32),
                   jax.ShapeDtypeStruct((rows, D_FF), f32), jax.ShapeDtypeStruct((rows, D_FF), f32)),
        grid=(1, nch),
        in_specs=[pl.BlockSpec((rows, D_MODEL), lambda i, c: (0, 0)),
                  pl.BlockSpec((rows, D_MODEL), lambda i, c: (0, 0)),
                  pl.BlockSpec((D_MODEL, fc), lambda i, c: (0, c)),
                  pl.BlockSpec((D_MODEL, fc), lambda i, c: (0, nch + c)),
                  pl.BlockSpec((CONV_W, fc), lambda i, c: (0, c)),
                  pl.BlockSpec((CONV_W, fc), lambda i, c: (0, nch + c)),
                  pl.BlockSpec((rows, fc), lambda i, c: (0, c)),
                  pl.BlockSpec((rows, fc), lambda i, c: (0, nch + c)),
                  pl.BlockSpec((fc, D_MODEL), lambda i, c: (c, 0)),
                  pl.BlockSpec((1, D_MODEL), lambda i, c: (0, 0))],
        out_specs=(pl.BlockSpec((rows, D_MODEL), lambda i, c: (0, 0)),
                   pl.BlockSpec((rows, fc), lambda i, c: (0, c)),
                   pl.BlockSpec((rows, fc), lambda i, c: (0, c))),
        scratch_shapes=[pltpu.VMEM((rows, D_MODEL), f32)],
        compiler_params=_cparams(("arbitrary", "arbitrary")),
        name="conv_ffn_small",
    )(h, x1, wup, wup, cw, cw, st, st, wd, gf)


def _rope_tables(pos):
    half = RET_DK // 2
    inv = ROPE_BASE ** (-jnp.arange(half, dtype=f32) / half)
    ang = pos.astype(f32)[:, None] * inv[None, :]
    c, s = jnp.cos(ang), jnp.sin(ang)
    return jnp.concatenate([c, c], axis=-1), jnp.concatenate([-s, s], axis=-1)


def _layer(xp, xs, ck, cv, page_table, s_ret, s_conv, meta, norm1_g, w_in, sb_bias, w_ret_proj, w_sb_proj, w_o,
           norm2_g, w_up, conv_w, w_down, norm_f_g):
    batch, seq, d = xp.shape
    nb, n_tok, _ = xs.shape
    n_pool, page = ck.shape[0], ck.shape[1]
    n_pages = page_table.shape[1]
    past = n_pages * page
    n_sample = nb * n_tok
    n_small = n_sample + N_META

    w_in, w_ret_proj, w_sb_proj, w_o, w_up, w_down = (
        w.astype(bf16) for w in (w_in, w_ret_proj, w_sb_proj, w_o, w_up, w_down))
    g1, g2, gf = norm1_g[None, :], norm2_g[None, :], norm_f_g[None, :]
    x_big = xp.reshape(batch * seq, d)
    x_small = jnp.concatenate([xs.reshape(n_sample, d), meta.astype(f32)], axis=0)

    tm_big = 1024
    proj_s = _norm_mm(x_small, g1, w_in, n_small, 1024)
    proj_p = _norm_mm(x_big, g1, w_in, tm_big, 1024)

    lg = jnp.log1p(-jnp.exp2(-5.0 - jnp.arange(RET_HEADS, dtype=f32)))
    lg = jnp.broadcast_to(lg[:, None, None], (RET_HEADS, 1, RET_DV))
    cos_m, sin_m = _rope_tables(jnp.arange(N_META))
    ret_m, s_meta = _ret_small(proj_s, n_sample, N_META, N_META, cos_m, sin_m, lg,
                               jnp.zeros((1, RET_HEADS, RET_DK, RET_DV), f32))
    cos_p, sin_p = _rope_tables(N_META + jnp.arange(seq))
    ret_p, sret_p = _ret_prompt(proj_p, cos_p, sin_p, lg, s_meta[0], batch, seq)
    cos_s, sin_s = _rope_tables(past + (jnp.arange(SMALL_ROWS) % n_tok))
    ret_s, sret_s = _ret_small(proj_s, 0, n_sample, n_tok, cos_s, sin_s, lg, s_ret)

    tri = (jnp.arange(SB_BK)[:, None] > jnp.arange(SB_BK)[None, :]).astype(bf16)
    pad_rows = lambda a, n: jnp.pad(a, [(0, 0)] * (a.ndim - 2) + [(0, n - a.shape[-2]), (0, 0)])
    km = pad_rows(proj_s[n_sample:, C_SK:C_SK + SB_W], 128).astype(bf16)
    vm = pad_rows(proj_s[n_sample:, C_SV:C_SV + SB_W], 128).astype(bf16)
    sb_m = _sb_meta(proj_s, n_sample, km, vm, sb_bias, tri)
    ps3 = proj_s[:n_sample].reshape(nb, n_tok, IN_WIDTH)
    cache_t = lambda c: jnp.transpose(c, (0, 2, 3, 1)).reshape(n_pool, SB_W, page)
    bias_col = jnp.repeat(sb_bias * LOG2E, n_tok)[:, None]
    sb_p, sb_s = _stick_breaking(proj_p, km, vm, sb_bias, tri, batch, seq,
                                 page_table, ps3, cache_t(ck), cache_t(cv), bias_col)

    ret_small = jnp.concatenate([ret_s, ret_m], axis=0)
    sb_small = jnp.concatenate([sb_s.reshape(n_sample, SB_W).astype(bf16), sb_m], axis=0)
    m_s = _gate_mix(ret_small, sb_small, proj_s, w_ret_proj, w_sb_proj, n_small, 512)
    m_p = _gate_mix(ret_p, sb_p, proj_p, w_ret_proj, w_sb_proj, tm_big, 512)
    x1_s, h2_s = _out_proj(x_small, m_s, w_o, g2, n_small)
    x1_p, h2_p = _out_proj(x_big, m_p, w_o, g2, 512)

    st = jnp.concatenate([jnp.pad(s_conv, ((0, 0), (0, n_tok - (CONV_W - 1)), (0, 0))).reshape(n_sample, 2 * D_FF),
                          jnp.zeros((N_META, 2 * D_FF), f32)], axis=0)
    y_s, ua_s, ug_s = _ffn_small(h2_s, x1_s, w_up, conv_w, st, w_down, gf, n_sample, n_tok)
    y_p, ta_p, tg_p = _ffn_prompt(h2_p, x1_p, w_up, conv_w, ua_s[n_small - 8:], ug_s[n_small - 8:], w_down, gf,
                                  tm_big, seq)

    y_prompt = y_p.reshape(batch, seq, d)
    y_sample = y_s[:n_sample].reshape(nb, n_tok, d)

    kt, vt = _kv_out(proj_p, proj_s, n_sample, batch, seq)
    token_major = lambda a: jnp.transpose(a.reshape(batch, SB_HEADS, SB_DH, N_META + seq), (0, 3, 1, 2))
    k_p, v_p = token_major(kt), token_major(vt)
    k_s = ps3[:, :, C_SK:C_SK + SB_W].reshape(nb, n_tok, SB_HEADS, SB_DH)
    v_s = ps3[:, :, C_SV:C_SV + SB_W].reshape(nb, n_tok, SB_HEADS, SB_DH)
    tiles_per_seq = seq // tm_big
    last = lambda tl: tl.reshape(batch, tiles_per_seq, 8, D_FF)[:, -1, 8 - (CONV_W - 1):, :]
    conv_p = jnp.concatenate([last(ta_p), last(tg_p)], axis=-1)
    u_s = jnp.concatenate([ua_s[:n_sample], ug_s[:n_sample]], axis=-1).reshape(nb, n_tok, 2 * D_FF)
    conv_s = u_s[:, n_tok - (CONV_W - 1):, :]
    return y_prompt, y_sample, k_p, v_p, k_s, v_s, sret_p, sret_s, conv_p, conv_s


def kernel(x_prompt, x_sample, cache_k, cache_v, page_table, state_ret, state_conv, meta_tokens, norm1_g, w_in,
           sb_bias, w_ret_proj, w_sb_proj, w_o, norm2_g, w_up, conv_w, w_down, norm_f_g):
    assert cache_k.shape[0] == 1, "one layer"
    outs = _layer(x_prompt, x_sample, cache_k[0], cache_v[0], page_table, state_ret[0], state_conv[0], meta_tokens,
                  norm1_g[0], w_in[0], sb_bias[0], w_ret_proj[0], w_sb_proj[0], w_o[0], norm2_g[0], w_up[0],
                  conv_w[0], w_down[0], norm_f_g)
    y_prompt, y_sample = outs[0], outs[1]
    return (y_prompt, y_sample) + tuple(o[None] for o in outs[2:])
```

```python
import functools

import jax
import jax.numpy as jnp
from jax import lax
from jax.experimental import pallas as pl
from jax.experimental.pallas import tpu as pltpu

f32 = jnp.float32
bf16 = jnp.bfloat16

D_MODEL = 2048
N_META = 16
RET_HEADS = 8
RET_DK = 128
RET_DV = 256
SB_HEADS = 16
SB_DH = 64
D_FF = 5632
CONV_W = 3
ROPE_BASE = 10000.0
EPS = 1e-6
GN_EPS = 1e-5

RET_QK = RET_HEADS * RET_DK
RET_V = RET_HEADS * RET_DV
SB_W = SB_HEADS * SB_DH
C_RQ, C_RK, C_RV, C_RG = 0, RET_QK, 2 * RET_QK, 2 * RET_QK + RET_V
C_SQ = C_RG + RET_V
C_SK, C_SV = C_SQ + SB_W, C_SQ + 2 * SB_W
C_GR = C_SV + SB_W
C_GS = C_GR + D_MODEL
IN_WIDTH = C_GS + D_MODEL

RET_CHUNK = 128
RET_HEAD_GROUP = 8
SB_BQ = 512
SB_BK = 256
LOG2E = 1.4426950408889634
SB_LOGIT_CLAMP = 100.0
SB_HEAD_GROUP = 4
SMALL_ROWS = 16
FF_CHUNK = 512
VMEM_LIMIT = 56 * 1024 * 1024


def _cparams(sem):
    return pltpu.CompilerParams(dimension_semantics=sem, vmem_limit_bytes=VMEM_LIMIT)


def _dot(a, b):
    return jnp.dot(a, b, preferred_element_type=f32)


def _dot_nt(a, b):
    return lax.dot_general(a, b, (((1,), (1,)), ((), ())), preferred_element_type=f32)


def _sigmoid(x):
    return 1.0 / (1.0 + jnp.exp(-x))


def _rmsnorm(x, g):
    return x * lax.rsqrt(jnp.mean(x * x, axis=-1, keepdims=True) + EPS) * g


def _norm_mm_kernel(x_ref, g_ref, w_ref, o_ref, h_ref):
    @pl.when(pl.program_id(1) == 0)
    def _():
        h_ref[...] = _rmsnorm(x_ref[...], g_ref[...]).astype(bf16)

    o_ref[...] = _dot(h_ref[...], w_ref[...])


def _norm_mm_cast_kernel(x_ref, g_ref, w_ref, o_ref, wb_ref, h_ref):
    @pl.when(pl.program_id(1) == 0)
    def _():
        h_ref[...] = _rmsnorm(x_ref[...], g_ref[...]).astype(bf16)

    wb = w_ref[...].astype(bf16)
    wb_ref[...] = wb
    o_ref[...] = _dot(h_ref[...], wb)


def _norm_mm_cast(x, g, w, tn):
    m, d = x.shape
    n = w.shape[1]
    return pl.pallas_call(
        _norm_mm_cast_kernel,
        out_shape=(jax.ShapeDtypeStruct((m, n), f32), jax.ShapeDtypeStruct((d, n), bf16)),
        grid=(1, n // tn),
        in_specs=[pl.BlockSpec((m, d), lambda i, j: (0, 0)),
                  pl.BlockSpec((1, d), lambda i, j: (0, 0)),
                  pl.BlockSpec((d, tn), lambda i, j: (0, j))],
        out_specs=(pl.BlockSpec((m, tn), lambda i, j: (0, j)), pl.BlockSpec((d, tn), lambda i, j: (0, j))),
        scratch_shapes=[pltpu.VMEM((m, d), bf16)],
        compiler_params=_cparams(("arbitrary", "arbitrary")),
        name="norm_in_proj_small",
    )(x, g, w)


def _norm_mm(x, g, w, tm, tn):
    m, d = x.shape
    n = w.shape[1]
    return pl.pallas_call(
        _norm_mm_kernel,
        out_shape=jax.ShapeDtypeStruct((m, n), f32),
        grid=(m // tm, n // tn),
        in_specs=[pl.BlockSpec((tm, d), lambda i, j: (i, 0)),
                  pl.BlockSpec((1, d), lambda i, j: (0, 0)),
                  pl.BlockSpec((d, tn), lambda i, j: (0, j))],
        out_specs=pl.BlockSpec((tm, tn), lambda i, j: (i, j)),
        scratch_shapes=[pltpu.VMEM((tm, d), bf16)],
        compiler_params=_cparams(("parallel", "arbitrary")),
        name="norm_in_proj",
    )(x, g, w)


def _rope(x, cos2, sin2):
    return x * cos2 + pltpu.roll(x, RET_DK // 2, 1) * sin2


def _gated_groupnorm(o, gate):
    mu = jnp.mean(o, axis=-1, keepdims=True)
    oc = o - mu
    var = jnp.mean(oc * oc, axis=-1, keepdims=True)
    return (gate * _sigmoid(gate)) * (oc * lax.rsqrt(var + GN_EPS))


def _ret_prompt_kernel(q_ref, k_ref, v_ref, g_ref, cos_ref, sin_ref, lg_ref, s0_ref, o_ref, sfin_ref, s_ref, dec_ref):
    c = pl.program_id(2)
    n = RET_CHUNK

    @pl.when(c == 0)
    def _():
        s_ref[...] = s0_ref[...]
        row = lax.broadcasted_iota(jnp.int32, (n, RET_DK), 0).astype(f32)
        diff = row - lax.broadcasted_iota(jnp.int32, (n, n), 1).astype(f32)
        for h in range(RET_HEAD_GROUP):
            lg = lg_ref[h][:, :RET_DK]
            dec_ref[h, 0] = jnp.where(diff >= 0, jnp.exp(diff * lg), 0.0)
            dec_ref[h, 1] = jnp.exp((row + 1.0) * lg)
            dec_ref[h, 2] = jnp.exp((n - 1.0 - row) * lg)

    cos2, sin2 = cos_ref[...], sin_ref[...]
    for h in range(RET_HEAD_GROUP):
        dk = slice(h * RET_DK, (h + 1) * RET_DK)
        dv = slice(h * RET_DV, (h + 1) * RET_DV)
        q = _rope(q_ref[:, dk], cos2, sin2)
        k = _rope(k_ref[:, dk], cos2, sin2) * (RET_DK ** -0.5)
        att = _dot_nt(q.astype(bf16), k.astype(bf16)) * dec_ref[h, 0]
        vb = v_ref[:, dv].astype(bf16)
        s = s_ref[h]
        o = _dot(att.astype(bf16), vb) + _dot((q * dec_ref[h, 1]).astype(bf16), s.astype(bf16))
        kd = k * dec_ref[h, 2]
        s_ref[h] = jnp.exp(n * lg_ref[h]) * s + _dot(kd.T.astype(bf16), vb)
        o_ref[:, dv] = _gated_groupnorm(o, g_ref[:, dv]).astype(bf16)

    @pl.when(c == pl.num_programs(2) - 1)
    def _():
        sfin_ref[...] = s_ref[...]


def _ret_prompt(proj, cos2, sin2, lg, s0, batch, seq):
    nc = seq // RET_CHUNK
    n = RET_CHUNK
    assert n == RET_DK, "the decay scratch stores (n, n) and (n, RET_DK) factors in one array"
    hg = RET_HEAD_GROUP
    wk, wv = hg * RET_DK, hg * RET_DV
    row = lambda b, h, c: b * nc + c
    return pl.pallas_call(
        _ret_prompt_kernel,
        out_shape=(jax.ShapeDtypeStruct((batch * seq, RET_V), bf16),
                   jax.ShapeDtypeStruct((batch, RET_HEADS, RET_DK, RET_DV), f32)),
        grid=(batch, RET_HEADS // hg, nc),
        in_specs=[pl.BlockSpec((n, wk), lambda b, h, c: (row(b, h, c), C_RQ // wk + h)),
                  pl.BlockSpec((n, wk), lambda b, h, c: (row(b, h, c), C_RK // wk + h)),
                  pl.BlockSpec((n, wv), lambda b, h, c: (row(b, h, c), C_RV // wv + h)),
                  pl.BlockSpec((n, wv), lambda b, h, c: (row(b, h, c), C_RG // wv + h)),
                  pl.BlockSpec((n, RET_DK), lambda b, h, c: (c, 0)),
                  pl.BlockSpec((n, RET_DK), lambda b, h, c: (c, 0)),
                  pl.BlockSpec((hg, 1, RET_DV), lambda b, h, c: (h, 0, 0)),
                  pl.BlockSpec((hg, RET_DK, RET_DV), lambda b, h, c: (h, 0, 0))],
        out_specs=(pl.BlockSpec((n, wv), lambda b, h, c: (row(b, h, c), h)),
                   pl.BlockSpec((None, hg, RET_DK, RET_DV), lambda b, h, c: (b, h, 0, 0))),
        scratch_shapes=[pltpu.VMEM((hg, RET_DK, RET_DV), f32), pltpu.VMEM((hg, 3, n, RET_DK), f32)],
        compiler_params=_cparams(("arbitrary", "arbitrary", "arbitrary")),
        name="retention_prompt",
    )(proj, proj, proj, proj, cos2, sin2, lg, s0)


def _ret_small_kernel(q_ref, k_ref, v_ref, g_ref, cos_ref, sin_ref, lg_ref, s0_ref, o_ref, s_ref, *, n_tok):
    r = SMALL_ROWS
    n_seq = r // n_tok
    ri = lax.broadcasted_iota(jnp.int32, (r, RET_DK), 0)
    ti = (ri % n_tok).astype(f32)
    rj = lax.broadcasted_iota(jnp.int32, (r, 128), 1)
    rr = lax.broadcasted_iota(jnp.int32, (r, 128), 0)
    tdiff = ((rr % n_tok) - (rj % n_tok)).astype(f32)
    visible = ((rr // n_tok) == (rj // n_tok)) & (tdiff >= 0)
    rk = lax.broadcasted_iota(jnp.int32, (128, RET_DK), 0)
    tk = (rk % n_tok).astype(f32)
    rv = lax.broadcasted_iota(jnp.int32, (r, RET_DV), 0)
    cos2, sin2 = cos_ref[...], sin_ref[...]
    pad = 128 - r
    for h in range(RET_HEADS):
        lgv = lg_ref[h]
        lg = lgv[:, :RET_DK]
        q = _rope(q_ref[:, h * RET_DK:(h + 1) * RET_DK], cos2, sin2)
        k = _rope(k_ref[:, h * RET_DK:(h + 1) * RET_DK], cos2, sin2) * (RET_DK ** -0.5)
        kpad = jnp.concatenate([k, jnp.zeros((pad, RET_DK), f32)], axis=0)
        vpad = jnp.concatenate([v_ref[:, h * RET_DV:(h + 1) * RET_DV], jnp.zeros((pad, RET_DV), f32)],
                               axis=0).astype(bf16)
        decay = jnp.where(visible, jnp.exp(tdiff * lg), 0.0)
        att = _dot_nt(q.astype(bf16), kpad.astype(bf16)) * decay
        o = _dot(att.astype(bf16), vpad)
        qd = (q * jnp.exp((ti + 1.0) * lg)).astype(bf16)
        kd = kpad * jnp.exp((n_tok - 1.0 - tk) * lg)
        for s_i in range(n_seq):
            s = s0_ref[s_i, h]
            mine_k = (rk // n_tok) == s_i
            mine_v = (rv // n_tok) == s_i
            o = o + jnp.where(mine_v, _dot(qd, s.astype(bf16)), 0.0)
            s_ref[s_i, h] = jnp.exp(n_tok * lgv) * s + _dot(jnp.where(mine_k, kd, 0.0).T.astype(bf16), vpad)
        o_ref[:, h * RET_DV:(h + 1) * RET_DV] = _gated_groupnorm(
            o, g_ref[:, h * RET_DV:(h + 1) * RET_DV]).astype(bf16)


def _ret_small(proj, row0, n_rows, n_tok, cos2, sin2, lg, s0):
    r = SMALL_ROWS
    n_seq = r // n_tok
    steps = n_rows // r
    b0 = row0 // r
    return pl.pallas_call(
        functools.partial(_ret_small_kernel, n_tok=n_tok),
        out_shape=(jax.ShapeDtypeStruct((n_rows, RET_V), bf16),
                   jax.ShapeDtypeStruct(s0.shape, f32)),
        grid=(steps,),
        in_specs=[pl.BlockSpec((r, RET_QK), lambda i: (b0 + i, C_RQ // RET_QK)),
                  pl.BlockSpec((r, RET_QK), lambda i: (b0 + i, C_RK // RET_QK)),
                  pl.BlockSpec((r, RET_V), lambda i: (b0 + i, C_RV // RET_V)),
                  pl.BlockSpec((r, RET_V), lambda i: (b0 + i, C_RG // RET_V)),
                  pl.BlockSpec((r, RET_DK), lambda i: (0, 0)),
                  pl.BlockSpec((r, RET_DK), lambda i: (0, 0)),
                  pl.BlockSpec((RET_HEADS, 1, RET_DV), lambda i: (0, 0, 0)),
                  pl.BlockSpec((n_seq, RET_HEADS, RET_DK, RET_DV), lambda i: (i, 0, 0, 0))],
        out_specs=(pl.BlockSpec((r, RET_V), lambda i: (i, 0)),
                   pl.BlockSpec((n_seq, RET_HEADS, RET_DK, RET_DV), lambda i: (i, 0, 0, 0))),
        compiler_params=_cparams(("parallel",)),
        name="retention_small",
    )(proj, proj, proj, proj, cos2, sin2, lg, s0)


def _sb_log2_keep(z2, mask):
    zc = jnp.minimum(z2, SB_LOGIT_CLAMP)
    lk = jnp.log(1.0 + jnp.exp2(zc)) * -LOG2E
    return zc, (lk if mask is None else jnp.where(mask, lk, 0.0))


def _sb_terms(z2, tri, mask):
    zc, lk = _sb_log2_keep(z2, mask)
    between = _dot(lk.astype(bf16), tri)
    w = jnp.exp2(lk + between + zc)
    if mask is not None:
        w = jnp.where(mask, w, 0.0)
    return w.astype(bf16), between[:, :1] + lk[:, :1]


def _sb_kernel(pt_ref, bias_ref, q_ref, k_ref, v_ref, km_ref, vm_ref, tri_ref, qs_ref, kn_ref, vn_ref, *rest,
               n_tok, n_grp):
    k_refs, v_refs = rest[:n_grp], rest[n_grp:2 * n_grp]
    bcol_ref, o_ref, os_ref, acc_ref, r_ref, mc_ref, mt_ref, qg_ref, accs_ref, rs_ref, new_ref = rest[2 * n_grp:]
    p, i = pl.program_id(1), pl.program_id(2)
    bq, bk = SB_BQ, SB_BK
    tri = tri_ref[...]

    page = k_refs[0].shape[1]
    rows = n_tok * SB_HEADS
    grp_rows = n_tok * SB_HEAD_GROUP
    grp_cols = SB_DH * SB_HEAD_GROUP
    n_row_grp = SB_HEADS // SB_HEAD_GROUP
    own = (lax.broadcasted_iota(jnp.int32, (grp_rows, grp_cols), 1) // SB_DH
           == lax.broadcasted_iota(jnp.int32, (grp_rows, grp_cols), 0) // n_tok)
    tri_p = tri[:page, :page]
    row_grp = lambda a, g: a[g * grp_rows:(g + 1) * grp_rows]
    col_grp = lambda g: slice(g * grp_cols, (g + 1) * grp_cols)
    by_row_grp = lambda f: jnp.concatenate([f(g) for g in range(n_row_grp)], axis=0)

    def sample_blocks(logits, weigh, mask, init):
        zc, lk = zip(*[_sb_log2_keep(z2 + bcol_ref[...], mask) for z2 in logits])
        between = _dot(jnp.concatenate([l.astype(bf16) for l in lk], axis=0), tri_p)
        r = jnp.zeros((rows, 1), f32) if init else rs_ref[...]
        acc = None if init else accs_ref[...]
        for g in range(len(logits)):
            btw = between[g * rows:(g + 1) * rows]
            w = jnp.exp2(lk[g] + btw + r + zc[g])
            if mask is not None:
                w = jnp.where(mask, w, 0.0)
            r = r + btw[:, :1] + lk[g][:, :1]
            contrib = weigh[g](w.astype(bf16))
            acc = contrib if acc is None else acc + contrib
        rs_ref[...] = r
        accs_ref[...] = acc

    @pl.when(i == 0)
    def _():
        qs = qs_ref[...] * (SB_DH ** -0.5 * LOG2E)
        for h in range(SB_HEADS):
            g = h // SB_HEAD_GROUP
            qg_ref[h * n_tok:(h + 1) * n_tok, :] = jnp.where(
                own[(h % SB_HEAD_GROUP) * n_tok:(h % SB_HEAD_GROUP + 1) * n_tok], qs[:, col_grp(g)], 0.0)
        new_ref[...] = jnp.zeros_like(new_ref)
        new_ref[0, :n_tok, :] = kn_ref[...]
        new_ref[1, :n_tok, :] = vn_ref[...]
        qg = qg_ref[...].astype(bf16)
        kk = lax.broadcasted_iota(jnp.int32, (rows, page), 1)
        tq = lax.broadcasted_iota(jnp.int32, (rows, page), 0) % n_tok
        z_new = by_row_grp(lambda g: _dot_nt(row_grp(qg, g), new_ref[0, :, col_grp(g)].astype(bf16)))
        wv_new = lambda w: by_row_grp(lambda g: _dot(row_grp(w, g), new_ref[1, :, col_grp(g)].astype(bf16)))
        sample_blocks([z_new], [wv_new], kk < tq, True)

    qg = qg_ref[...].astype(bf16)
    sample_blocks(
        [by_row_grp(lambda g, r_=r_: _dot(row_grp(qg, g), r_[col_grp(g), :].astype(bf16))) for r_ in k_refs],
        [lambda w, r_=r_: by_row_grp(lambda g: _dot_nt(row_grp(w, g), r_[col_grp(g), :].astype(bf16)))
         for r_ in v_refs], None, False)

    q = q_ref[...] * (SB_DH ** -0.5 * LOG2E)
    lane = lax.broadcasted_iota(jnp.int32, (bq, 2 * SB_DH), 1)
    first = lane < SB_DH
    qm = (jnp.where(first, q, 0.0).astype(bf16), jnp.where(first, 0.0, q).astype(bf16))
    bias = (bias_ref[2 * p] * LOG2E, bias_ref[2 * p + 1] * LOG2E)

    def terms(kb, vb, mask, tri_b):
        out = []
        for hh in range(2):
            w, total = _sb_terms(_dot_nt(qm[hh], kb) + bias[hh], tri_b, mask)
            out.append((_dot(w, vb), total))
        return out

    def accumulate(res):
        for hh, (contrib, total) in enumerate(res):
            r = r_ref[hh]
            acc_ref[hh] += jnp.exp2(r) * contrib
            r_ref[hh] = r + total

    def kv(j):
        s0 = pl.multiple_of(j * bk, bk)
        return k_ref[pl.ds(s0, bk), :].astype(bf16), v_ref[pl.ds(s0, bk), :].astype(bf16)

    nkb = bq // bk
    kk = lax.broadcasted_iota(jnp.int32, (bq, bk), 1)
    qq = lax.broadcasted_iota(jnp.int32, (bq, bk), 0)
    for d in reversed(range(nkb)):
        res = terms(*kv(i * nkb + d), kk + d * bk < qq, tri)
        if d == nkb - 1:
            for hh, (contrib, total) in enumerate(res):
                acc_ref[hh] = contrib
                r_ref[hh] = total
        else:
            accumulate(res)

    nm = km_ref.shape[0]
    km_valid = lax.broadcasted_iota(jnp.int32, (bq, nm), 1) < N_META
    for hh, (contrib, total) in enumerate(terms(km_ref[...], vm_ref[...], km_valid, tri[:nm, :nm])):
        mc_ref[hh] = contrib
        mt_ref[hh] = total

    def pair(t, carry):
        newer = terms(*kv(i * nkb - 1 - 2 * t), None, tri)
        older = terms(*kv(i * nkb - 2 - 2 * t), None, tri)
        accumulate(newer)
        accumulate(older)
        return carry

    lax.fori_loop(0, i * (nkb // 2), pair, 0)

    accumulate([(mc_ref[hh], mt_ref[hh]) for hh in range(2)])
    o_ref[...] = jnp.where(first, acc_ref[0], acc_ref[1]).astype(bf16)

    @pl.when(i == pl.num_programs(2) - 1)
    def _():
        for g in range(n_row_grp):
            sel = jnp.where(own, row_grp(accs_ref[...], g), 0.0)
            os_ref[:, col_grp(g)] = sum(sel[hh * n_tok:(hh + 1) * n_tok] for hh in range(SB_HEAD_GROUP))


def _stick_breaking(proj, km, vm, bias, tri, batch, seq, page_table, proj3, ck, cv, bias_col):
    nq = seq // SB_BQ
    hp = 2 * SB_DH
    n_pairs = SB_HEADS // 2
    nb, n_tok, _ = proj3.shape
    n_pages = page_table.shape[1]
    page = ck.shape[2]
    rows = n_tok * SB_HEADS
    assert nb == batch * n_pairs and n_pages % nq == 0
    n_grp = n_pages // nq

    def page_spec(g):
        return pl.BlockSpec((None, SB_W, page),
                            lambda b, p, i, pt, bs: (pt[b * n_pairs + p, n_pages - 1 - (i * n_grp + g)], 0, 0))

    new_tok = lambda col: pl.BlockSpec((None, n_tok, SB_W), lambda b, p, i, pt, bs: (b * n_pairs + p, 0, col // SB_W))
    return pl.pallas_call(
        functools.partial(_sb_kernel, n_tok=n_tok, n_grp=n_grp),
        out_shape=(jax.ShapeDtypeStruct((batch * seq, SB_W), bf16), jax.ShapeDtypeStruct((nb, n_tok, SB_W), f32)),
        grid_spec=pltpu.PrefetchScalarGridSpec(
            num_scalar_prefetch=2,
            grid=(batch, n_pairs, nq),
            in_specs=([pl.BlockSpec((SB_BQ, hp), lambda b, p, i, pt, bs: (b * nq + i, C_SQ // hp + p)),
                       pl.BlockSpec((seq, hp), lambda b, p, i, pt, bs: (b, C_SK // hp + p)),
                       pl.BlockSpec((seq, hp), lambda b, p, i, pt, bs: (b, C_SV // hp + p)),
                       pl.BlockSpec((km.shape[0], hp), lambda b, p, i, pt, bs: (0, p)),
                       pl.BlockSpec((km.shape[0], hp), lambda b, p, i, pt, bs: (0, p)),
                       pl.BlockSpec((SB_BK, SB_BK), lambda b, p, i, pt, bs: (0, 0)),
                       new_tok(C_SQ), new_tok(C_SK), new_tok(C_SV)]
                      + [page_spec(g) for g in range(n_grp)] + [page_spec(g) for g in range(n_grp)]
                      + [pl.BlockSpec((rows, 1), lambda b, p, i, pt, bs: (0, 0))]),
            out_specs=(pl.BlockSpec((SB_BQ, hp), lambda b, p, i, pt, bs: (b * nq + i, p)),
                       new_tok(0)),
            scratch_shapes=[pltpu.VMEM((2, SB_BQ, hp), f32), pltpu.VMEM((2, SB_BQ, 1), f32),
                            pltpu.VMEM((2, SB_BQ, hp), f32), pltpu.VMEM((2, SB_BQ, 1), f32),
                            pltpu.VMEM((rows, SB_DH * SB_HEAD_GROUP), f32),
                            pltpu.VMEM((rows, SB_DH * SB_HEAD_GROUP), f32),
                            pltpu.VMEM((rows, 1), f32), pltpu.VMEM((2, page, SB_W), f32)]),
        compiler_params=_cparams(("arbitrary", "arbitrary", "arbitrary")),
        name="stick_breaking",
    )(page_table, bias, proj, proj, proj, km, vm, tri, proj3, proj3, proj3,
      *([ck] * n_grp), *([cv] * n_grp), bias_col)


def _sb_meta_kernel(bias_ref, q_ref, km_ref, vm_ref, tri_ref, o_ref):
    p = pl.program_id(0)
    nq, nm = q_ref.shape[0], km_ref.shape[0]
    q = q_ref[...] * (SB_DH ** -0.5 * LOG2E)
    first = lax.broadcasted_iota(jnp.int32, (nq, 2 * SB_DH), 1) < SB_DH
    mask = lax.broadcasted_iota(jnp.int32, (nq, nm), 1) < lax.broadcasted_iota(jnp.int32, (nq, nm), 0)
    tri = tri_ref[...][:nm, :nm]
    kb, vb = km_ref[...], vm_ref[...]
    out = []
    for hh, qh in enumerate((jnp.where(first, q, 0.0), jnp.where(first, 0.0, q))):
        z2 = _dot_nt(qh.astype(bf16), kb) + bias_ref[2 * p + hh] * LOG2E
        w, _ = _sb_terms(z2, tri, mask)
        out.append(_dot(w, vb))
    o_ref[...] = jnp.where(first, out[0], out[1]).astype(bf16)


def _sb_meta(proj_small, row0, km, vm, bias, tri):
    hp = 2 * SB_DH
    nm = km.shape[0]
    return pl.pallas_call(
        _sb_meta_kernel,
        out_shape=jax.ShapeDtypeStruct((N_META, SB_W), bf16),
        grid_spec=pltpu.PrefetchScalarGridSpec(
            num_scalar_prefetch=1,
            grid=(SB_HEADS // 2,),
            in_specs=[pl.BlockSpec((N_META, hp), lambda p, s: (row0 // N_META, C_SQ // hp + p)),
                      pl.BlockSpec((nm, hp), lambda p, s: (0, p)),
                      pl.BlockSpec((nm, hp), lambda p, s: (0, p)),
                      pl.BlockSpec((SB_BK, SB_BK), lambda p, s: (0, 0))],
            out_specs=pl.BlockSpec((N_META, hp), lambda p, s: (0, p))),
        compiler_params=_cparams(("parallel",)),
        name="stick_breaking_meta",
    )(bias, proj_small, km, vm, tri)


def _kv_out_kernel(mk_ref, k_ref, mv_ref, v_ref, ok_ref, ov_ref):
    seq = k_ref.shape[0]
    lanes = k_ref.shape[1]
    for m_ref, x_ref, o_ref in ((mk_ref, k_ref, ok_ref), (mv_ref, v_ref, ov_ref)):
        o_ref[:, 0:lanes] = jnp.concatenate([m_ref[...], x_ref[0:lanes - N_META, :]], axis=0).T
        for t in range(1, seq // lanes):
            o_ref[:, t * lanes:(t + 1) * lanes] = x_ref[t * lanes - N_META:(t + 1) * lanes - N_META, :].T
        last = jnp.concatenate([x_ref[seq - N_META:seq, :], jnp.zeros((lanes - N_META, lanes), f32)], axis=0)
        o_ref[:, seq:seq + N_META] = last.T[:, :N_META]


def _kv_out(proj, proj_small, meta_row0, batch, seq):
    lanes = 128
    total = N_META + seq
    meta = lambda col: pl.BlockSpec((N_META, lanes), lambda b, c: (meta_row0 // N_META, col // lanes + c))
    rows = lambda col: pl.BlockSpec((seq, lanes), lambda b, c: (b, col // lanes + c))
    out = pl.BlockSpec((None, lanes, total), lambda b, c: (b, c, 0))
    return pl.pallas_call(
        _kv_out_kernel,
        out_shape=(jax.ShapeDtypeStruct((batch, SB_W, total), f32), jax.ShapeDtypeStruct((batch, SB_W, total), f32)),
        grid=(batch, SB_W // lanes),
        in_specs=[meta(C_SK), rows(C_SK), meta(C_SV), rows(C_SV)],
        out_specs=(out, out),
        compiler_params=_cparams(("parallel", "parallel")),
        name="kv_prompt_out",
    )(proj_small, proj, proj_small, proj)


def _gate_mix_kernel(ret_ref, sb_ref, gr_ref, gs_ref, wr_ref, ws_ref, o_ref):
    m = _sigmoid(gr_ref[...]) * _dot(ret_ref[...], wr_ref[...]) + _sigmoid(gs_ref[...]) * _dot(sb_ref[...], ws_ref[...])
    o_ref[...] = m.astype(bf16)


def _gate_mix(ret, sb, proj, wr, ws, tm, tn):
    m = ret.shape[0]
    return pl.pallas_call(
        _gate_mix_kernel,
        out_shape=jax.ShapeDtypeStruct((m, D_MODEL), bf16),
        grid=(m // tm, D_MODEL // tn),
        in_specs=[pl.BlockSpec((tm, RET_V), lambda i, j: (i, 0)),
                  pl.BlockSpec((tm, SB_W), lambda i, j: (i, 0)),
                  pl.BlockSpec((tm, tn), lambda i, j: (i, C_GR // tn + j)),
                  pl.BlockSpec((tm, tn), lambda i, j: (i, C_GS // tn + j)),
                  pl.BlockSpec((RET_V, tn), lambda i, j: (0, j)),
                  pl.BlockSpec((SB_W, tn), lambda i, j: (0, j))],
        out_specs=pl.BlockSpec((tm, tn), lambda i, j: (i, j)),
        compiler_params=_cparams(("parallel", "arbitrary")),
        name="gate_mix",
    )(ret, sb, proj, proj, wr, ws)


def _out_proj_kernel(x_ref, m_ref, wo_ref, g_ref, x1_ref, h_ref):
    x1 = x_ref[...] + _dot(m_ref[...], wo_ref[...])
    x1_ref[...] = x1
    h_ref[...] = _rmsnorm(x1, g_ref[...]).astype(bf16)


def _out_proj(x, m, wo, g, tm):
    rows = x.shape[0]
    return pl.pallas_call(
        _out_proj_kernel,
        out_shape=(jax.ShapeDtypeStruct((rows, D_MODEL), f32), jax.ShapeDtypeStruct((rows, D_MODEL), bf16)),
        grid=(rows // tm,),
        in_specs=[pl.BlockSpec((tm, D_MODEL), lambda i: (i, 0)),
                  pl.BlockSpec((tm, D_MODEL), lambda i: (i, 0)),
                  pl.BlockSpec((D_MODEL, D_MODEL), lambda i: (0, 0)),
                  pl.BlockSpec((1, D_MODEL), lambda i: (0, 0))],
        out_specs=(pl.BlockSpec((tm, D_MODEL), lambda i: (i, 0)), pl.BlockSpec((tm, D_MODEL), lambda i: (i, 0))),
        compiler_params=_cparams(("parallel",)),
        name="out_proj_norm",
    )(x, m, wo, g)


def _gelu_tanh(x):
    return 0.5 * x * (1.0 + jnp.tanh(0.7978845608028654 * (x + 0.044715 * (x * x * x))))


def _ffn_finish(c, act, wd_ref, x_ref, gf_ref, y_ref, acc_ref):
    contrib = _dot(act.astype(bf16), wd_ref[...])

    @pl.when(c == 0)
    def _():
        acc_ref[...] = contrib

    @pl.when(c > 0)
    def _():
        acc_ref[...] += contrib

    @pl.when(c == pl.num_programs(1) - 1)
    def _():
        y_ref[...] = _rmsnorm(x_ref[...] + acc_ref[...], gf_ref[...])


def _ffn_prompt_kernel(h_ref, x_ref, wa_ref, wg_ref, cwa_ref, cwg_ref, pa_ref, pg_ref, wd_ref, gf_ref, y_ref, ta_ref, tg_ref,
                       prev_ref, act_ref, *, tiles_per_seq, n_chunks):
    i, c = pl.program_id(0), pl.program_id(1)
    tm = h_ref.shape[0]
    seq_start = (i % tiles_per_seq) == 0
    row = lax.broadcasted_iota(jnp.int32, (8, FF_CHUNK), 0)

    def conv_half(u, cw_ref, first_ref, slot, tail_ref):
        prev = jnp.where(seq_start, first_ref[...], prev_ref[slot])
        r1, r2 = pltpu.roll(u, 1, 0), pltpu.roll(u, 2, 0)
        u1 = jnp.concatenate([jnp.where(row == 0, prev[7:8, :], r1[:8])] + [r1[8:]], axis=0)
        u2 = jnp.concatenate([jnp.where(row == 0, prev[6:7, :], jnp.where(row == 1, prev[7:8, :], r2[:8]))]
                             + [r2[8:]], axis=0)
        cw = cw_ref[...]
        tail = u[tm - 8:, :]
        prev_ref[slot] = tail
        tail_ref[...] = tail
        return cw[2:3, :] * u + cw[1:2, :] * u1 + cw[0:1, :] * u2

    def up_matmuls():
        h = h_ref[...]
        return _dot(h, wa_ref[...]), _dot(h, wg_ref[...])

    def activation(ua, ug):
        a = conv_half(ua, cwa_ref, pa_ref, c, ta_ref)
        g = conv_half(ug, cwg_ref, pg_ref, n_chunks + c, tg_ref)
        act_ref[c % 2] = (_gelu_tanh(a) * g).astype(bf16)

    def down():
        return _dot(act_ref[(c + 1) % 2], wd_ref[...])

    @pl.when(c == 0)
    def _():
        y_ref[...] = jnp.zeros_like(y_ref)
        activation(*up_matmuls())

    @pl.when((c > 0) & (c < n_chunks))
    def _():
        y_ref[...] += down()
        activation(*up_matmuls())

    @pl.when(c == n_chunks)
    def _():
        y_ref[...] = _rmsnorm(x_ref[...] + y_ref[...] + down(), gf_ref[...])


def _ffn_prompt(h, x1, wup, cw, first_a, first_g, wd, gf, tm, rows_per_seq):
    rows = h.shape[0]
    fc = FF_CHUNK
    nch = D_FF // fc
    nt = rows // tm
    up_c = lambda c: jnp.minimum(c, nch - 1)
    return pl.pallas_call(
        functools.partial(_ffn_prompt_kernel, tiles_per_seq=rows_per_seq // tm, n_chunks=nch),
        out_shape=(jax.ShapeDtypeStruct((rows, D_MODEL), f32),
                   jax.ShapeDtypeStruct((nt, 8, D_FF), f32), jax.ShapeDtypeStruct((nt, 8, D_FF), f32)),
        grid=(nt, nch + 1),
        in_specs=[pl.BlockSpec((tm, D_MODEL), lambda i, c: (i, 0)),
                  pl.BlockSpec((tm, D_MODEL), lambda i, c: (i, 0), pipeline_mode=pl.Buffered(1)),
                  pl.BlockSpec((D_MODEL, fc), lambda i, c: (0, up_c(c))),
                  pl.BlockSpec((D_MODEL, fc), lambda i, c: (0, nch + up_c(c))),
                  pl.BlockSpec((CONV_W, fc), lambda i, c: (0, up_c(c))),
                  pl.BlockSpec((CONV_W, fc), lambda i, c: (0, nch + up_c(c))),
                  pl.BlockSpec((8, fc), lambda i, c: (0, up_c(c))),
                  pl.BlockSpec((8, fc), lambda i, c: (0, up_c(c))),
                  pl.BlockSpec((fc, D_MODEL), lambda i, c: (jnp.maximum(c - 1, 0), 0)),
                  pl.BlockSpec((1, D_MODEL), lambda i, c: (0, 0))],
        out_specs=(pl.BlockSpec((tm, D_MODEL), lambda i, c: (i, 0)),
                   pl.BlockSpec((None, 8, fc), lambda i, c: (i, 0, up_c(c))),
                   pl.BlockSpec((None, 8, fc), lambda i, c: (i, 0, up_c(c)))),
        scratch_shapes=[pltpu.VMEM((2 * nch, 8, fc), f32), pltpu.VMEM((2, tm, fc), bf16)],
        compiler_params=_cparams(("arbitrary", "arbitrary")),
        name="conv_ffn_prompt",
    )(h, x1, wup, wup, cw, cw, first_a, first_g, wd, gf)


def _ffn_small_kernel(h_ref, x_ref, wa_ref, wg_ref, cwa_ref, cwg_ref, sta_ref, stg_ref, wd_ref, gf_ref,
                      y_ref, ua_ref, ug_ref, acc_ref, *, n_sample, n_tok):
    c = pl.program_id(1)
    rows = h_ref.shape[0]
    h = h_ref[...]
    row = lax.broadcasted_iota(jnp.int32, (rows, FF_CHUNK), 0)
    t = jnp.where(row < n_sample, row % n_tok, row - n_sample)

    def conv_half(w_ref, cw_ref, st_ref, u_ref):
        u = _dot(h, w_ref[...])
        u_ref[...] = u
        st = st_ref[...]
        u1 = jnp.where(t == 0, pltpu.roll(st, rows - 1, 0), pltpu.roll(u, 1, 0))
        u2 = jnp.where(t < 2, st, pltpu.roll(u, 2, 0))
        cw = cw_ref[...]
        return cw[2:3, :] * u + cw[1:2, :] * u1 + cw[0:1, :] * u2

    a = conv_half(wa_ref, cwa_ref, sta_ref, ua_ref)
    g = conv_half(wg_ref, cwg_ref, stg_ref, ug_ref)
    _ffn_finish(c, _gelu_tanh(a) * g, wd_ref, x_ref, gf_ref, y_ref, acc_ref)


def _ffn_small(h, x1, wup, cw, st, wd, gf, n_sample, n_tok):
    rows = h.shape[0]
    fc = FF_CHUNK
    nch = D_FF // fc
    return pl.pallas_call(
        functools.partial(_ffn_small_kernel, n_sample=n_sample, n_tok=n_tok),
        out_shape=(jax.ShapeDtypeStruct((rows, D_MODEL), f32),
                   jax.ShapeDtypeStruct((rows, D_FF), f32), jax.ShapeDtypeStruct((rows, D_FF), f32)),
        grid=(1, nch),
        in_specs=[pl.BlockSpec((rows, D_MODEL), lambda i, c: (0, 0)),
                  pl.BlockSpec((rows, D_MODEL), lambda i, c: (0, 0)),
                  pl.BlockSpec((D_MODEL, fc), lambda i, c: (0, c)),
                  pl.BlockSpec((D_MODEL, fc), lambda i, c: (0, nch + c)),
                  pl.BlockSpec((CONV_W, fc), lambda i, c: (0, c)),
                  pl.BlockSpec((CONV_W, fc), lambda i, c: (0, nch + c)),
                  pl.BlockSpec((rows, fc), lambda i, c: (0, c)),
                  pl.BlockSpec((rows, fc), lambda i, c: (0, nch + c)),
                  pl.BlockSpec((fc, D_MODEL), lambda i, c: (c, 0)),
                  pl.BlockSpec((1, D_MODEL), lambda i, c: (0, 0))],
        out_specs=(pl.BlockSpec((rows, D_MODEL), lambda i, c: (0, 0)),
                   pl.BlockSpec((rows, fc), lambda i, c: (0, c)),
                   pl.BlockSpec((rows, fc), lambda i, c: (0, c))),
        scratch_shapes=[pltpu.VMEM((rows, D_MODEL), f32)],
        compiler_params=_cparams(("arbitrary", "arbitrary")),
        name="conv_ffn_small",
    )(h, x1, wup, wup, cw, cw, st, st, wd, gf)


def _rope_tables(pos):
    half = RET_DK // 2
    inv = ROPE_BASE ** (-jnp.arange(half, dtype=f32) / half)
    ang = pos.astype(f32)[:, None] * inv[None, :]
    c, s = jnp.cos(ang), jnp.sin(ang)
    return jnp.concatenate([c, c], axis=-1), jnp.concatenate([-s, s], axis=-1)


def _layer(xp, xs, ck, cv, page_table, s_ret, s_conv, meta, norm1_g, w_in, sb_bias, w_ret_proj, w_sb_proj, w_o,
           norm2_g, w_up, conv_w, w_down, norm_f_g):
    batch, seq, d = xp.shape
    nb, n_tok, _ = xs.shape
    n_pool, page = ck.shape[0], ck.shape[1]
    n_pages = page_table.shape[1]
    past = n_pages * page
    n_sample = nb * n_tok
    n_small = n_sample + N_META

    w_ret_proj, w_sb_proj, w_o, w_up, w_down = (
        w.astype(bf16) for w in (w_ret_proj, w_sb_proj, w_o, w_up, w_down))
    g1, g2, gf = norm1_g[None, :], norm2_g[None, :], norm_f_g[None, :]
    x_big = xp.reshape(batch * seq, d)
    x_small = jnp.concatenate([xs.reshape(n_sample, d), meta.astype(f32)], axis=0)

    tm_big = 1024
    proj_s, w_in = _norm_mm_cast(x_small, g1, w_in, 1024)
    proj_p = _norm_mm(x_big, g1, w_in, tm_big, 1024)

    lg = jnp.log1p(-jnp.exp2(-5.0 - jnp.arange(RET_HEADS, dtype=f32)))
    lg = jnp.broadcast_to(lg[:, None, None], (RET_HEADS, 1, RET_DV))
    cos_m, sin_m = _rope_tables(jnp.arange(N_META))
    ret_m, s_meta = _ret_small(proj_s, n_sample, N_META, N_META, cos_m, sin_m, lg,
                               jnp.zeros((1, RET_HEADS, RET_DK, RET_DV), f32))
    cos_p, sin_p = _rope_tables(N_META + jnp.arange(seq))
    ret_p, sret_p = _ret_prompt(proj_p, cos_p, sin_p, lg, s_meta[0], batch, seq)
    cos_s, sin_s = _rope_tables(past + (jnp.arange(SMALL_ROWS) % n_tok))
    ret_s, sret_s = _ret_small(proj_s, 0, n_sample, n_tok, cos_s, sin_s, lg, s_ret)

    tri = (jnp.arange(SB_BK)[:, None] > jnp.arange(SB_BK)[None, :]).astype(bf16)
    pad_rows = lambda a, n: jnp.pad(a, [(0, 0)] * (a.ndim - 2) + [(0, n - a.shape[-2]), (0, 0)])
    km = pad_rows(proj_s[n_sample:, C_SK:C_SK + SB_W], 128).astype(bf16)
    vm = pad_rows(proj_s[n_sample:, C_SV:C_SV + SB_W], 128).astype(bf16)
    sb_m = _sb_meta(proj_s, n_sample, km, vm, sb_bias, tri)
    ps3 = proj_s[:n_sample].reshape(nb, n_tok, IN_WIDTH)
    cache_t = lambda c: jnp.transpose(c, (0, 2, 3, 1)).reshape(n_pool, SB_W, page)
    bias_col = jnp.repeat(sb_bias * LOG2E, n_tok)[:, None]
    sb_p, sb_s = _stick_breaking(proj_p, km, vm, sb_bias, tri, batch, seq,
                                 page_table, ps3, cache_t(ck), cache_t(cv), bias_col)

    ret_small = jnp.concatenate([ret_s, ret_m], axis=0)
    sb_small = jnp.concatenate([sb_s.reshape(n_sample, SB_W).astype(bf16), sb_m], axis=0)
    m_s = _gate_mix(ret_small, sb_small, proj_s, w_ret_proj, w_sb_proj, n_small, 512)
    m_p = _gate_mix(ret_p, sb_p, proj_p, w_ret_proj, w_sb_proj, tm_big, 512)
    x1_s, h2_s = _out_proj(x_small, m_s, w_o, g2, n_small)
    x1_p, h2_p = _out_proj(x_big, m_p, w_o, g2, 512)

    st = jnp.concatenate([jnp.pad(s_conv, ((0, 0), (0, n_tok - (CONV_W - 1)), (0, 0))).reshape(n_sample, 2 * D_FF),
                          jnp.zeros((N_META, 2 * D_FF), f32)], axis=0)
    y_s, ua_s, ug_s = _ffn_small(h2_s, x1_s, w_up, conv_w, st, w_down, gf, n_sample, n_tok)
    y_p, ta_p, tg_p = _ffn_prompt(h2_p, x1_p, w_up, conv_w, ua_s[n_small - 8:], ug_s[n_small - 8:], w_down, gf,
                                  tm_big, seq)

    y_prompt = y_p.reshape(batch, seq, d)
    y_sample = y_s[:n_sample].reshape(nb, n_tok, d)

    kt, vt = _kv_out(proj_p, proj_s, n_sample, batch, seq)
    token_major = lambda a: jnp.transpose(a.reshape(batch, SB_HEADS, SB_DH, N_META + seq), (0, 3, 1, 2))
    k_p, v_p = token_major(kt), token_major(vt)
    k_s = ps3[:, :, C_SK:C_SK + SB_W].reshape(nb, n_tok, SB_HEADS, SB_DH)
    v_s = ps3[:, :, C_SV:C_SV + SB_W].reshape(nb, n_tok, SB_HEADS, SB_DH)
    tiles_per_seq = seq // tm_big
    last = lambda tl: tl.reshape(batch, tiles_per_seq, 8, D_FF)[:, -1, 8 - (CONV_W - 1):, :]
    conv_p = jnp.concatenate([last(ta_p), last(tg_p)], axis=-1)
    u_s = jnp.concatenate([ua_s[:n_sample], ug_s[:n_sample]], axis=-1).reshape(nb, n_tok, 2 * D_FF)
    conv_s = u_s[:, n_tok - (CONV_W - 1):, :]
    return y_prompt, y_sample, k_p, v_p, k_s, v_s, sret_p, sret_s, conv_p, conv_s


def kernel(x_prompt, x_sample, cache_k, cache_v, page_table, state_ret, state_conv, meta_tokens, norm1_g, w_in,
           sb_bias, w_ret_proj, w_sb_proj, w_o, norm2_g, w_up, conv_w, w_down, norm_f_g):
    assert cache_k.shape[0] == 1, "one layer"
    outs = _layer(x_prompt, x_sample, cache_k[0], cache_v[0], page_table, state_ret[0], state_conv[0], meta_tokens,
                  norm1_g[0], w_in[0], sb_bias[0], w_ret_proj[0], w_sb_proj[0], w_o[0], norm2_g[0], w_up[0],
                  conv_w[0], w_down[0], norm_f_g)
    y_prompt, y_sample = outs[0], outs[1]
    return (y_prompt, y_sample) + tuple(o[None] for o in outs[2:])
```
